```python
import math
import jax, jax.numpy as jnp
from jax import lax
import numpy as np

D_MODEL = 2048
BATCH = 1
SEQ = 16384
DEPTH = 2
DEC_BATCH = 4
DEC_SEQ = 8192
PAST_LEN = 128

GRID_W = 64
N_POOL_GROUPS = 4
POOL_WINDOWS = (2, 4, 8, 16)
GROUP_DIM = D_MODEL // N_POOL_GROUPS
N_HEADS = 16
HEAD_DIM = D_MODEL // N_HEADS
WIN_H = 8
WIN_W = 16
D_FF = 5632
CONV_W = 3
N_MIXERS = 2
N_POOL_LAYERS = (DEPTH + 1) // 2
N_ATTN_LAYERS = DEPTH // 2
EPS = 1e-6
NEG_INF = -1e30

kernel_name = "hybrid_pool_natten_convffn_encoder"


def rmsnorm(x, g):
    xf = x.astype(jnp.float32)
    y = xf * lax.rsqrt(jnp.mean(xf * xf, axis=-1, keepdims=True) + EPS)
    return (y * g.astype(jnp.float32)).astype(x.dtype)


def multiscale_pool_mixer(u, w, b, scale):
    B, S, _ = u.shape
    ug = u.reshape(B, S, N_POOL_GROUPS, GROUP_DIM)
    t = np.arange(S)
    outs = []
    for g, wsz in enumerate(POOL_WINDOWS):
        xg = ug[:, :, g, :]
        xf = xg.astype(jnp.float32)
        cs = jnp.concatenate([jnp.zeros((B, 1, GROUP_DIM), jnp.float32),
                              jnp.cumsum(xf, axis=1)], axis=1)
        lo = np.clip(t - wsz // 2, 0, S)
        hi = np.clip(t - wsz // 2 + wsz, 0, S)
        cnt = jnp.asarray((hi - lo).astype(np.float32))[None, :, None]
        mean = (jnp.take(cs, jnp.asarray(hi), axis=1) - jnp.take(cs, jnp.asarray(lo), axis=1)) / cnt
        d = (mean - xf).astype(u.dtype)
        outs.append(d @ w[g] + b[g])
    return jnp.concatenate(outs, axis=-1) * scale


def neighbourhood_attention(u, w_qkv, b_qkv, rpb, w_o):
    B, S, _ = u.shape
    rows = S // GRID_W
    kh = min(WIN_H, rows)
    qkv = u @ w_qkv + b_qkv
    q, k, v = jnp.split(qkv, 3, axis=-1)
    q = q.reshape(B, rows, GRID_W, N_HEADS, HEAD_DIM) * (HEAD_DIM ** -0.5)
    k = k.reshape(B, rows, GRID_W, N_HEADS, HEAD_DIM)
    v = v.reshape(B, rows, GRID_W, N_HEADS, HEAD_DIM)

    c = np.arange(GRID_W)
    cstart = np.clip(c - WIN_W // 2, 0, GRID_W - WIN_W)
    col_mask = jnp.asarray((c[None, :] >= cstart[:, None]) & (c[None, :] < cstart[:, None] + WIN_W))
    col_idx = jnp.asarray(np.clip(c[None, :] - c[:, None], -(WIN_W - 1), WIN_W - 1) + WIN_W - 1)

    def row_step(r):
        rs = jnp.clip(r - kh // 2, 0, rows - kh)
        k_band = lax.dynamic_slice_in_dim(k, rs, kh, axis=1)
        v_band = lax.dynamic_slice_in_dim(v, rs, kh, axis=1)
        q_row = lax.dynamic_index_in_dim(q, r, axis=1, keepdims=False)
        s = jnp.einsum('bchd,brkhd->bhcrk', q_row, k_band).astype(jnp.float32)
        row_idx = rs + jnp.arange(kh) - r + (WIN_H - 1)
        bias = rpb[:, row_idx[None, :, None], col_idx[:, None, :]]
        s = s + bias.astype(jnp.float32)[None]
        s = jnp.where(col_mask[None, None, :, None, :], s, NEG_INF)
        p = jax.nn.softmax(s.reshape(B, N_HEADS, GRID_W, kh * GRID_W), axis=-1)
        p = p.reshape(B, N_HEADS, GRID_W, kh, GRID_W).astype(v.dtype)
        return jnp.einsum('bhcrk,brkhd->bchd', p, v_band)

    o = lax.map(row_step, jnp.arange(rows))
    o = jnp.moveaxis(o, 0, 1).reshape(B, S, D_MODEL)
    return o @ w_o


def conv_ffn(u, w_up, conv_w, conv_b, w_down):
    h = u @ w_up
    gate, val = jnp.split(h, 2, axis=-1)
    gp = jnp.pad(gate, ((0, 0), (1, 1), (0, 0)))
    gate = gp[:, :-2] * conv_w[0] + gp[:, 1:-1] * conv_w[1] + gp[:, 2:] * conv_w[2] + conv_b
    return (jax.nn.gelu(gate, approximate=False) * val) @ w_down


def trunk(x, mix_norm, pool_w, pool_b, pool_scale, attn_w_qkv, attn_b_qkv, attn_rpb, attn_w_o,
          ffn_norm, ffn_w_up, ffn_conv_w, ffn_conv_b, ffn_w_down, final_norm):
    for i in range(DEPTH):
        u = rmsnorm(x, mix_norm[i])
        j = i // N_MIXERS
        if i % N_MIXERS == 0:
            x = x + multiscale_pool_mixer(u, pool_w[j], pool_b[j], pool_scale[j])
        else:
            x = x + neighbourhood_attention(u, attn_w_qkv[j], attn_b_qkv[j], attn_rpb[j], attn_w_o[j])
        u = rmsnorm(x, ffn_norm[i])
        x = x + conv_ffn(u, ffn_w_up[i], ffn_conv_w[i], ffn_conv_b[i], ffn_w_down[i])
    return rmsnorm(x, final_norm)


def setup_inputs(seed: int = 0) -> dict:
    key = jax.random.key(seed)
    ks = jax.random.split(key, 20)
    f32 = jnp.float32
    n = lambda k, s, sc: jax.random.normal(k, s, f32) * sc
    return {
        "x_prompt": n(ks[0], (BATCH, SEQ, D_MODEL), 1.0),
        "x_sample": n(ks[1], (DEC_BATCH, DEC_SEQ, D_MODEL), 1.0),
        "mix_norm": 1.0 + n(ks[2], (DEPTH, D_MODEL), 0.02),
        "pool_w": n(ks[3], (N_POOL_LAYERS, N_POOL_GROUPS, GROUP_DIM, GROUP_DIM), GROUP_DIM ** -0.5),
        "pool_b": n(ks[4], (N_POOL_LAYERS, N_POOL_GROUPS, GROUP_DIM), 0.02),
        "pool_scale": 1.0 + n(ks[5], (N_POOL_LAYERS, D_MODEL), 0.02),
        "attn_w_qkv": n(ks[6], (N_ATTN_LAYERS, D_MODEL, 3 * D_MODEL), D_MODEL ** -0.5),
        "attn_b_qkv": n(ks[7], (N_ATTN_LAYERS, 3 * D_MODEL), 0.02),
        "attn_rpb": n(ks[8], (N_ATTN_LAYERS, N_HEADS, 2 * WIN_H - 1, 2 * WIN_W - 1), 0.1),
        "attn_w_o": n(ks[9], (N_ATTN_LAYERS, D_MODEL, D_MODEL), D_MODEL ** -0.5),
        "ffn_norm": 1.0 + n(ks[10], (DEPTH, D_MODEL), 0.02),
        "ffn_w_up": n(ks[11], (DEPTH, D_MODEL, 2 * D_FF), D_MODEL ** -0.5),
        "ffn_conv_w": n(ks[12], (DEPTH, CONV_W, D_FF), CONV_W ** -0.5),
        "ffn_conv_b": n(ks[13], (DEPTH, D_FF), 0.02),
        "ffn_w_down": n(ks[14], (DEPTH, D_FF, D_MODEL), D_FF ** -0.5),
        "final_norm": 1.0 + n(ks[15], (D_MODEL,), 0.02),
    }


def reference(x_prompt, x_sample, mix_norm, pool_w, pool_b, pool_scale, attn_w_qkv, attn_b_qkv,
              attn_rpb, attn_w_o, ffn_norm, ffn_w_up, ffn_conv_w, ffn_conv_b, ffn_w_down, final_norm):
    y_prompt = trunk(x_prompt, mix_norm, pool_w, pool_b, pool_scale, attn_w_qkv, attn_b_qkv,
                     attn_rpb, attn_w_o, ffn_norm, ffn_w_up, ffn_conv_w, ffn_conv_b, ffn_w_down,
                     final_norm)
    y_sample = trunk(x_sample, mix_norm, pool_w, pool_b, pool_scale, attn_w_qkv, attn_b_qkv,
                     attn_rpb, attn_w_o, ffn_norm, ffn_w_up, ffn_conv_w, ffn_conv_b, ffn_w_down,
                     final_norm)
    return (y_prompt, y_sample)
```

```python
import functools

import numpy as np
import jax
import jax.numpy as jnp
from jax import lax
from jax.experimental import pallas as pl
from jax.experimental.pallas import tpu as pltpu

D_MODEL = 2048
GRID_W = 64
N_POOL_GROUPS = 4
POOL_WINDOWS = (2, 4, 8, 16)
GROUP_DIM = D_MODEL // N_POOL_GROUPS
N_HEADS = 16
HEAD_DIM = D_MODEL // N_HEADS
WIN_H = 8
WIN_W = 16
D_FF = 5632
EPS = 1e-6
NEG_INF = -1e30

HALO = 8
POOL_TOKENS = 512
FFN_TOKENS = 1024
FFN_CHUNK = 512
QKV_TOKENS = 1024
QKV_CHUNK = 1024
WO_TOKENS = 512
ATT_ROWS = 8
ATT_KEY_ROWS = ATT_ROWS + WIN_H
VMEM_LIMIT = 60 * 1024 * 1024

F32 = jnp.float32
BF16 = jnp.bfloat16


def _rms(x, g):
    return x * lax.rsqrt(jnp.mean(x * x, axis=-1, keepdims=True) + EPS) * g


def _dot(a, b):
    return jnp.dot(a, b, preferred_element_type=F32)


def _params(sem):
    return pltpu.CompilerParams(dimension_semantics=sem, vmem_limit_bytes=VMEM_LIMIT)


def _halo_specs(tokens, seq):
    per = tokens // HALO
    last = seq // HALO - 1
    main = pl.BlockSpec((1, tokens, D_MODEL), lambda b, i, *_: (b, i, 0))
    prev = pl.BlockSpec((1, HALO, D_MODEL), lambda b, i, *_: (b, jnp.maximum(i * per - 1, 0), 0))
    nxt = pl.BlockSpec((1, HALO, D_MODEL), lambda b, i, *_: (b, jnp.minimum((i + 1) * per, last), 0))
    return main, prev, nxt


def _pool_kernel(xm_ref, xp_ref, xn_ref, g_ref, w_ref, b_ref, sc_ref, o_ref, *, seq, tp):
    i = pl.program_id(1)
    n = tp + 2 * HALO
    xm = xm_ref[0]
    xa = jnp.concatenate([xp_ref[0], xm, xn_ref[0]], axis=0)
    pos = i * tp - HALO + lax.broadcasted_iota(jnp.int32, (n, 1), 0)
    u = jnp.where((pos >= 0) & (pos < seq), _rms(xa, g_ref[...]), 0.0)
    posm = pos[HALO:HALO + tp]
    for g, w in enumerate(POOL_WINDOWS):
        cols = slice(g * GROUP_DIM, (g + 1) * GROUP_DIM)
        ug = u[:, cols]
        s = ug + pltpu.roll(ug, 1, 0)
        width = 2
        while width < w:
            half = width // 2
            s = pltpu.roll(s, half, 0) + pltpu.roll(s, n - half, 0)
            width *= 2
        lo = jnp.clip(posm - w // 2, 0, seq)
        hi = jnp.clip(posm - w // 2 + w, 0, seq)
        cnt = (hi - lo).astype(F32)
        d = s[HALO:HALO + tp] / cnt - ug[HALO:HALO + tp]
        y = _dot(d.astype(BF16), w_ref[g]) + b_ref[g]
        o_ref[0, :, cols] = xm[:, cols] + y * sc_ref[:, cols]


def _pool_layer(x, g, w, b, sc):
    B, S, D = x.shape
    tp = min(POOL_TOKENS, S)
    main, prev, nxt = _halo_specs(tp, S)
    const2 = lambda b_, i: (0, 0)
    const3 = lambda b_, i: (0, 0, 0)
    return pl.pallas_call(
        functools.partial(_pool_kernel, seq=S, tp=tp),
        grid=(B, S // tp),
        in_specs=[main, prev, nxt,
                  pl.BlockSpec((1, D), const2),
                  pl.BlockSpec((N_POOL_GROUPS, GROUP_DIM, GROUP_DIM), const3),
                  pl.BlockSpec((N_POOL_GROUPS, 1, GROUP_DIM), const3),
                  pl.BlockSpec((1, D), const2)],
        out_specs=pl.BlockSpec((1, tp, D), lambda b_, i: (b_, i, 0)),
        out_shape=jax.ShapeDtypeStruct((B, S, D), F32),
        compiler_params=_params(("parallel", "parallel")),
        name="pool_layer",
    )(x, x, x, g, w, b, sc)


def _gelu(x):
    return 0.5 * x * (1.0 + lax.erf(x * np.float32(np.sqrt(0.5))))


def _ffn_kernel(xm_ref, xp_ref, xn_ref, nw_ref, wg_ref, wv_ref, cw_ref, cb_ref, wd_ref, fn_ref,
                o_ref, u_ref, *, tm, final_norm):
    i = pl.program_id(1)
    j = pl.program_id(2)
    n = tm + 2 * HALO

    @pl.when(j == 0)
    def _():
        xm = xm_ref[0]
        g = nw_ref[...]
        u_ref[0:tm] = _rms(xm, g).astype(BF16)
        un = jnp.where(i < pl.num_programs(1) - 1, _rms(xn_ref[0], g), 0.0)
        up = jnp.where(i > 0, _rms(xp_ref[0], g), 0.0)
        u_ref[tm:n] = jnp.concatenate([un, up], axis=0).astype(BF16)
        o_ref[0] = xm

    gate = _dot(u_ref[...], wg_ref[0])
    val = _dot(u_ref[0:tm], wv_ref[0])
    cw = cw_ref[0]
    conv = (pltpu.roll(gate, 1, 0)[:tm] * cw[0:1] + gate[:tm] * cw[1:2]
            + pltpu.roll(gate, n - 1, 0)[:tm] * cw[2:3] + cb_ref[0])
    h = _gelu(conv) * val
    o_ref[0] += _dot(h.astype(BF16), wd_ref[0])

    if final_norm:
        @pl.when(j == pl.num_programs(2) - 1)
        def _():
            o_ref[0] = _rms(o_ref[0], fn_ref[...])


def _ffn_layer(x, nw, wg, wv, cw, cb, wd, fn, *, final_norm):
    B, S, D = x.shape
    tm = min(FFN_TOKENS, S)
    nj, _, tf = wg.shape
    main, prev, nxt = _halo_specs(tm, S)
    const2 = lambda b_, i, j: (0, 0)
    chunk = lambda b_, i, j: (j, 0, 0)
    return pl.pallas_call(
        functools.partial(_ffn_kernel, tm=tm, final_norm=final_norm),
        grid=(B, S // tm, nj),
        in_specs=[main, prev, nxt,
                  pl.BlockSpec((1, D), const2),
                  pl.BlockSpec((1, D, tf), chunk),
                  pl.BlockSpec((1, D, tf), chunk),
                  pl.BlockSpec((1, 3, tf), chunk),
                  pl.BlockSpec((1, 1, tf), chunk),
                  pl.BlockSpec((1, tf, D), chunk),
                  pl.BlockSpec((1, D), const2)],
        out_specs=pl.BlockSpec((1, tm, D), lambda b_, i, j: (b_, i, 0)),
        out_shape=jax.ShapeDtypeStruct((B, S, D), F32),
        scratch_shapes=[pltpu.VMEM((tm + 2 * HALO, D), BF16)],
        compiler_params=_params(("parallel", "parallel", "arbitrary")),
        name="ffn_final" if final_norm else "ffn",
    )(x, x, x, nw, wg, wv, cw, cb, wd, fn)


def _qkv_kernel(x_ref, nw_ref, w_ref, b_ref, s_ref, o_ref, u_ref):
    @pl.when(pl.program_id(2) == 0)
    def _():
        u_ref[...] = _rms(x_ref[0], nw_ref[...]).astype(BF16)

    o_ref[0] = ((_dot(u_ref[...], w_ref[...]) + b_ref[...]) * s_ref[...]).astype(BF16)


def _qkv_layer(x, nw, w, b, s):
    B, S, D = x.shape
    tm = min(QKV_TOKENS, S)
    tn = QKV_CHUNK
    N = w.shape[1]
    return pl.pallas_call(
        _qkv_kernel,
        grid=(B, S // tm, N // tn),
        in_specs=[pl.BlockSpec((1, tm, D), lambda b_, i, j: (b_, i, 0)),
                  pl.BlockSpec((1, D), lambda b_, i, j: (0, 0)),
                  pl.BlockSpec((D, tn), lambda b_, i, j: (0, j)),
                  pl.BlockSpec((1, tn), lambda b_, i, j: (0, j)),
                  pl.BlockSpec((1, tn), lambda b_, i, j: (0, j))],
        out_specs=pl.BlockSpec((1, tm, tn), lambda b_, i, j: (b_, i, j)),
        out_shape=jax.ShapeDtypeStruct((B, S, N), BF16),
        scratch_shapes=[pltpu.VMEM((tm, D), BF16)],
        compiler_params=_params(("parallel", "parallel", "arbitrary")),
        name="qkv",
    )(x, nw, w, b, s)


def _attn_bias_tables(rpb):
    rows = 4 * ATT_ROWS
    kh = WIN_H
    c = np.arange(GRID_W)
    cstart = np.clip(c - WIN_W // 2, 0, GRID_W - WIN_W)
    col_ok = (c[None, :] >= cstart[:, None]) & (c[None, :] < cstart[:, None] + WIN_W)
    col_idx = np.clip(c[None, :] - c[:, None], -(WIN_W - 1), WIN_W - 1) + WIN_W - 1
    tables = []
    for r0 in (0, ATT_ROWS, rows - ATT_ROWS):
        r = r0 + np.arange(ATT_ROWS)
        rs = np.clip(r - kh // 2, 0, rows - kh)
        kr = r0 - kh // 2 + np.arange(ATT_KEY_ROWS)
        row_ok = (kr[None, :] >= rs[:, None]) & (kr[None, :] < rs[:, None] + kh)
        row_idx = np.clip(kr[None, :] - r[:, None] + (WIN_H - 1), 0, 2 * WIN_H - 2)
        ok = row_ok[:, None, :, None] & col_ok[None, :, None, :]
        ri = np.broadcast_to(row_idx[:, None, :, None], ok.shape)
        ci = np.broadcast_to(col_idx[None, :, None, :], ok.shape)
        bias = rpb[:, ri, ci].astype(F32)
        bias = jnp.where(ok[None], bias, NEG_INF)
        tables.append(bias.reshape(N_HEADS, ATT_ROWS * GRID_W, ATT_KEY_ROWS * GRID_W))
    return jnp.stack(tables)


def _attn_kernel(q_ref, kp_ref, kc_ref, kn_ref, vp_ref, vc_ref, vn_ref, bias_ref, o_ref):
    q = q_ref[0]
    k = jnp.concatenate([kp_ref[0], kc_ref[0], kn_ref[0]], axis=0)
    v = jnp.concatenate([vp_ref[0], vc_ref[0], vn_ref[0]], axis=0)
    s = lax.dot_general(q, k, (((1,), (1,)), ((), ())), preferred_element_type=F32)
    s = s + bias_ref[0, 0]
    p = jnp.exp(s - jnp.max(s, axis=-1, keepdims=True))
    l = jnp.sum(p, axis=-1, keepdims=True)
    o_ref[0] = (_dot(p.astype(BF16), v) / l).astype(BF16)


def _attn_layer(qkv, bias):
    B, S, _ = qkv.shape
    tq = ATT_ROWS * GRID_W
    th = tq // 2
    nblk = S // tq
    assert nblk >= 3
    last_h = S // th - 1
    H = N_HEADS

    def cur(off):
        return pl.BlockSpec((1, tq, HEAD_DIM), lambda b_, h, i: (b_, i, off + h))

    def prev(off):
        return pl.BlockSpec((1, th, HEAD_DIM), lambda b_, h, i: (b_, jnp.maximum(2 * i - 1, 0), off + h))

    def nxt(off):
        return pl.BlockSpec((1, th, HEAD_DIM), lambda b_, h, i: (b_, jnp.minimum(2 * i + 2, last_h), off + h))

    def variant(b_, h, i):
        return (jnp.where(i == 0, 0, jnp.where(i == nblk - 1, 2, 1)), h, 0, 0)

    return pl.pallas_call(
        _attn_kernel,
        grid=(B, H, nblk),
        in_specs=[cur(0), prev(H), cur(H), nxt(H), prev(2 * H), cur(2 * H), nxt(2 * H),
                  pl.BlockSpec((1, 1, tq, ATT_KEY_ROWS * GRID_W), variant)],
        out_specs=pl.BlockSpec((1, tq, HEAD_DIM), lambda b_, h, i: (b_, i, h)),
        out_shape=jax.ShapeDtypeStruct((B, S, D_MODEL), BF16),
        compiler_params=_params(("parallel", "parallel", "arbitrary")),
        name="natten",
    )(qkv, qkv, qkv, qkv, qkv, qkv, qkv, bias)


def _wo_kernel(o_ref, x_ref, w_ref, y_ref):
    y_ref[0] = x_ref[0] + _dot(o_ref[0], w_ref[...])


def _wo_layer(o, x, w):
    B, S, D = x.shape
    tm = min(WO_TOKENS, S)
    tile = pl.BlockSpec((1, tm, D), lambda b_, i: (b_, i, 0))
    return pl.pallas_call(
        _wo_kernel,
        grid=(B, S // tm),
        in_specs=[tile, tile, pl.BlockSpec((D, D), lambda b_, i: (0, 0))],
        out_specs=tile,
        out_shape=jax.ShapeDtypeStruct((B, S, D), F32),
        compiler_params=_params(("parallel", "parallel")),
        name="attn_out",
    )(o, x, w)


def _prep_weights(mix_norm, pool_w, pool_b, pool_scale, attn_w_qkv, attn_b_qkv, attn_rpb, attn_w_o,
                  ffn_norm, ffn_w_up, ffn_conv_w, ffn_conv_b, ffn_w_down, final_norm):
    tf = FFN_CHUNK
    nj = D_FF // tf
    ffn = []
    for l in range(ffn_w_up.shape[0]):
        up = ffn_w_up[l].astype(BF16)
        wg = up[:, :D_FF].reshape(D_MODEL, nj, tf).transpose(1, 0, 2)
        wv = up[:, D_FF:].reshape(D_MODEL, nj, tf).transpose(1, 0, 2)
        wd = ffn_w_down[l].astype(BF16).reshape(nj, tf, D_MODEL)
        cw = ffn_conv_w[l].reshape(3, nj, tf).transpose(1, 0, 2)
        cb = ffn_conv_b[l].reshape(nj, 1, tf)
        ffn.append((ffn_norm[l].reshape(1, D_MODEL), wg, wv, cw, cb, wd))
    q_scale = jnp.concatenate([jnp.full((D_MODEL,), HEAD_DIM ** -0.5, F32),
                               jnp.ones((2 * D_MODEL,), F32)]).reshape(1, 3 * D_MODEL)
    return dict(
        mix_norm=mix_norm.reshape(-1, 1, D_MODEL),
        pool_w=pool_w[0].astype(BF16),
        pool_b=pool_b[0].reshape(N_POOL_GROUPS, 1, GROUP_DIM),
        pool_scale=pool_scale[0].reshape(1, D_MODEL),
        w_qkv=attn_w_qkv[0].astype(BF16),
        b_qkv=attn_b_qkv[0].reshape(1, 3 * D_MODEL),
        q_scale=q_scale,
        bias=_attn_bias_tables(attn_rpb[0]),
        w_o=attn_w_o[0].astype(BF16),
        ffn=ffn,
        final_norm=final_norm.reshape(1, D_MODEL),
    )


def _trunk(x, p):
    x = _pool_layer(x, p["mix_norm"][0], p["pool_w"], p["pool_b"], p["pool_scale"])
    x = _ffn_layer(x, *p["ffn"][0], p["final_norm"], final_norm=False)
    qkv = _qkv_layer(x, p["mix_norm"][1], p["w_qkv"], p["b_qkv"], p["q_scale"])
    o = _attn_layer(qkv, p["bias"])
    x = _wo_layer(o, x, p["w_o"])
    return _ffn_layer(x, *p["ffn"][1], p["final_norm"], final_norm=True)


def kernel(x_prompt, x_sample, mix_norm, pool_w, pool_b, pool_scale, attn_w_qkv, attn_b_qkv, attn_rpb,
           attn_w_o, ffn_norm, ffn_w_up, ffn_conv_w, ffn_conv_b, ffn_w_down, final_norm):
    p = _prep_weights(mix_norm, pool_w, pool_b, pool_scale, attn_w_qkv, attn_b_qkv, attn_rpb, attn_w_o,
                      ffn_norm, ffn_w_up, ffn_conv_w, ffn_conv_b, ffn_w_down, final_norm)
    return _trunk(x_prompt, p), _trunk(x_sample, p)
```

```python
import functools

import numpy as np
import jax
import jax.numpy as jnp
from jax import lax
from jax.experimental import pallas as pl
from jax.experimental.pallas import tpu as pltpu

D_MODEL = 2048
GRID_W = 64
N_POOL_GROUPS = 4
POOL_WINDOWS = (2, 4, 8, 16)
GROUP_DIM = D_MODEL // N_POOL_GROUPS
N_HEADS = 16
HEAD_DIM = D_MODEL // N_HEADS
WIN_H = 8
WIN_W = 16
D_FF = 5632
EPS = 1e-6
NEG_INF = -1e30

LANES = 128
HALO = 8
POOL_TOKENS = 512
FFN_TOKENS = 1024
FFN_CHUNK = 512
QKV_TOKENS = 1024
QKV_CHUNK = 1024
WO_TOKENS = 512
ATT_ROWS = 8
ATT_HEADS = 4
ATT_PAIRS = ATT_ROWS // 2
ATT_WIN_ROWS = WIN_H + 2
ATT_KEY_ROWS = ATT_ROWS + WIN_H
VMEM_LIMIT = 60 * 1024 * 1024

F32 = jnp.float32
BF16 = jnp.bfloat16


def _rms(x, g):
    return x * lax.rsqrt(jnp.mean(x * x, axis=-1, keepdims=True) + EPS) * g


def _dot(a, b):
    return jnp.dot(a, b, preferred_element_type=F32)


def _params(sem):
    return pltpu.CompilerParams(dimension_semantics=sem, vmem_limit_bytes=VMEM_LIMIT)


def _halo_specs(tokens, seq):
    per = tokens // HALO
    last = seq // HALO - 1
    main = pl.BlockSpec((1, tokens, D_MODEL), lambda b, i, *_: (b, i, 0))
    prev = pl.BlockSpec((1, HALO, D_MODEL), lambda b, i, *_: (b, jnp.maximum(i * per - 1, 0), 0))
    nxt = pl.BlockSpec((1, HALO, D_MODEL), lambda b, i, *_: (b, jnp.minimum((i + 1) * per, last), 0))
    return main, prev, nxt


def _pool_kernel(xm_ref, xp_ref, xn_ref, g_ref, w_ref, b_ref, sc_ref, o_ref, *, seq, tp):
    i = pl.program_id(1)
    n = tp + 2 * HALO
    xm = xm_ref[0]
    xa = jnp.concatenate([xp_ref[0], xm, xn_ref[0]], axis=0)
    pos = i * tp - HALO + lax.broadcasted_iota(jnp.int32, (n, 1), 0)
    u = jnp.where((pos >= 0) & (pos < seq), _rms(xa, g_ref[...]), 0.0)
    posm = pos[HALO:HALO + tp]
    for g, w in enumerate(POOL_WINDOWS):
        cols = slice(g * GROUP_DIM, (g + 1) * GROUP_DIM)
        ug = u[:, cols]
        s = ug + pltpu.roll(ug, 1, 0)
        width = 2
        while width < w:
            half = width // 2
            s = pltpu.roll(s, half, 0) + pltpu.roll(s, n - half, 0)
            width *= 2
        lo = jnp.clip(posm - w // 2, 0, seq)
        hi = jnp.clip(posm - w // 2 + w, 0, seq)
        cnt = (hi - lo).astype(F32)
        d = s[HALO:HALO + tp] / cnt - ug[HALO:HALO + tp]
        y = _dot(d.astype(BF16), w_ref[g]) + b_ref[g]
        o_ref[0, :, cols] = xm[:, cols] + y * sc_ref[:, cols]


def _pool_layer(x, g, w, b, sc):
    B, S, D = x.shape
    tp = min(POOL_TOKENS, S)
    main, prev, nxt = _halo_specs(tp, S)
    const2 = lambda b_, i: (0, 0)
    const3 = lambda b_, i: (0, 0, 0)
    return pl.pallas_call(
        functools.partial(_pool_kernel, seq=S, tp=tp),
        grid=(B, S // tp),
        in_specs=[main, prev, nxt,
                  pl.BlockSpec((1, D), const2),
                  pl.BlockSpec((N_POOL_GROUPS, GROUP_DIM, GROUP_DIM), const3),
                  pl.BlockSpec((N_POOL_GROUPS, 1, GROUP_DIM), const3),
                  pl.BlockSpec((1, D), const2)],
        out_specs=pl.BlockSpec((1, tp, D), lambda b_, i: (b_, i, 0)),
        out_shape=jax.ShapeDtypeStruct((B, S, D), F32),
        compiler_params=_params(("parallel", "parallel")),
        name="pool_layer",
    )(x, x, x, g, w, b, sc)


def _gelu(x):
    return 0.5 * x * (1.0 + lax.erf(x * np.float32(np.sqrt(0.5))))


def _ffn_kernel(xm_ref, xp_ref, xn_ref, nw_ref, wg_ref, wv_ref, cw_ref, cb_ref, wd_ref, fn_ref,
                o_ref, u_ref, *, tm, final_norm):
    i = pl.program_id(1)
    j = pl.program_id(2)
    n = tm + 2 * HALO

    @pl.when(j == 0)
    def _():
        xm = xm_ref[0]
        g = nw_ref[0]
        u_ref[0:tm] = _rms(xm, g).astype(BF16)
        un = jnp.where(i < pl.num_programs(1) - 1, _rms(xn_ref[0], g), 0.0)
        up = jnp.where(i > 0, _rms(xp_ref[0], g), 0.0)
        u_ref[tm:n] = jnp.concatenate([un, up], axis=0).astype(BF16)
        o_ref[0] = xm

    gate = _dot(u_ref[...], wg_ref[0])
    val = _dot(u_ref[0:tm], wv_ref[0])
    cw = cw_ref[0]
    conv = (pltpu.roll(gate, 1, 0)[:tm] * cw[0:1] + gate[:tm] * cw[1:2]
            + pltpu.roll(gate, n - 1, 0)[:tm] * cw[2:3] + cb_ref[0])
    h = _gelu(conv) * val
    o_ref[0] += _dot(h.astype(BF16), wd_ref[0])

    if final_norm:
        @pl.when(j == pl.num_programs(2) - 1)
        def _():
            o_ref[0] = _rms(o_ref[0], fn_ref[...])


def _ffn_layer(x, layer, nw, w_up, cw, cb, w_down, fn, *, final_norm):
    B, S, D = x.shape
    tm = min(FFN_TOKENS, S)
    tf = FFN_CHUNK
    nj = D_FF // tf
    main, prev, nxt = _halo_specs(tm, S)
    return pl.pallas_call(
        functools.partial(_ffn_kernel, tm=tm, final_norm=final_norm),
        grid=(B, S // tm, nj),
        in_specs=[main, prev, nxt,
                  pl.BlockSpec((1, 1, D), lambda b_, i, j: (layer, 0, 0)),
                  pl.BlockSpec((1, D, tf), lambda b_, i, j: (layer, 0, j)),
                  pl.BlockSpec((1, D, tf), lambda b_, i, j: (layer, 0, nj + j)),
                  pl.BlockSpec((1, 3, tf), lambda b_, i, j: (layer, 0, j)),
                  pl.BlockSpec((1, 1, tf), lambda b_, i, j: (layer, 0, j)),
                  pl.BlockSpec((1, tf, D), lambda b_, i, j: (layer, j, 0)),
                  pl.BlockSpec((1, D), lambda b_, i, j: (0, 0))],
        out_specs=pl.BlockSpec((1, tm, D), lambda b_, i, j: (b_, i, 0)),
        out_shape=jax.ShapeDtypeStruct((B, S, D), F32),
        scratch_shapes=[pltpu.VMEM((tm + 2 * HALO, D), BF16)],
        compiler_params=_params(("parallel", "parallel", "arbitrary")),
        name="ffn_final" if final_norm else "ffn",
    )(x, x, x, nw, w_up, w_up, cw, cb, w_down, fn)


def _qkv_kernel(x_ref, nw_ref, w_ref, b_ref, s_ref, o_ref, u_ref):
    @pl.when(pl.program_id(2) == 0)
    def _():
        u_ref[...] = _rms(x_ref[0], nw_ref[...]).astype(BF16)

    res = ((_dot(u_ref[...], w_ref[...]) + b_ref[...]) * s_ref[...]).astype(BF16)
    for hh in range(o_ref.shape[1]):
        o_ref[0, hh] = res[:, hh * HEAD_DIM:(hh + 1) * HEAD_DIM]


def _qkv_layer(x, nw, w, b, s):
    B, S, D = x.shape
    tm = min(QKV_TOKENS, S)
    tn = QKV_CHUNK
    N = w.shape[1]
    return pl.pallas_call(
        _qkv_kernel,
        grid=(B, S // tm, N // tn),
        in_specs=[pl.BlockSpec((1, tm, D), lambda b_, i, j: (b_, i, 0)),
                  pl.BlockSpec((1, D), lambda b_, i, j: (0, 0)),
                  pl.BlockSpec((D, tn), lambda b_, i, j: (0, j)),
                  pl.BlockSpec((1, tn), lambda b_, i, j: (0, j)),
                  pl.BlockSpec((1, tn), lambda b_, i, j: (0, j))],
        out_specs=pl.BlockSpec((1, tn // HEAD_DIM, tm, HEAD_DIM), lambda b_, i, j: (b_, j, i, 0)),
        out_shape=jax.ShapeDtypeStruct((B, N // HEAD_DIM, S, HEAD_DIM), BF16),
        scratch_shapes=[pltpu.VMEM((tm, D), BF16)],
        compiler_params=_params(("parallel", "parallel", "arbitrary")),
        name="qkv",
    )(x, nw, w, b, s)


def _pair_patterns():
    rows = 4 * ATT_ROWS
    masked = 2 * WIN_H - 1
    tables, starts, tids = {}, np.zeros((3, ATT_PAIRS), int), np.zeros((3, ATT_PAIRS), int)
    for v, r0 in enumerate((0, ATT_ROWS, rows - ATT_ROWS)):
        for m in range(ATT_PAIRS):
            r = r0 + 2 * m + np.arange(2)
            kstart = np.clip(r - WIN_H // 2, 0, rows - WIN_H) - (r0 - WIN_H // 2)
            w0 = min(kstart[0] // 2, (ATT_KEY_ROWS - ATT_WIN_ROWS) // 2)
            kk = 2 * w0 + np.arange(ATT_WIN_ROWS)
            ok = (kk[None, :] >= kstart[:, None]) & (kk[None, :] < kstart[:, None] + WIN_H)
            assert ok.sum() == 2 * WIN_H
            dr = (r0 - WIN_H // 2 + kk)[None, :] - r[:, None] + WIN_H - 1
            idx = np.where(ok, dr, masked)
            assert idx.min() >= 0 and idx.max() <= masked
            starts[v, m] = w0
            tids[v, m] = tables.setdefault(tuple(idx.ravel()), len(tables))
    return [np.array(t).reshape(2, ATT_WIN_ROWS) for t in tables], starts, tids


_PAIR_TABLES, _PAIR_STARTS, _PAIR_TIDS = _pair_patterns()


def _attn_bias_tables(rpb):
    H = rpb.shape[0]
    c = np.arange(GRID_W)
    cstart = np.clip(c - WIN_W // 2, 0, GRID_W - WIN_W)
    col_ok = (c[None, :] >= cstart[:, None]) & (c[None, :] < cstart[:, None] + WIN_W)
    span = 2 * GRID_W
    vi = np.clip(np.arange(span) - (GRID_W - 1), -(WIN_W - 1), WIN_W - 1) + WIN_W - 1
    vp = rpb[:, :, vi]
    skew = jnp.broadcast_to(vp[:, :, None, :], (H, 2 * WIN_H - 1, GRID_W, span))
    skew = skew.reshape(H, 2 * WIN_H - 1, GRID_W * span)[:, :, :GRID_W * (span - 1)]
    t = skew.reshape(H, 2 * WIN_H - 1, GRID_W, span - 1)[..., GRID_W - 1:]
    t = jnp.where(col_ok, t, NEG_INF)
    t = jnp.concatenate([t, jnp.full((H, 1, GRID_W, GRID_W), NEG_INF, F32)], axis=1)
    tabs = []
    for idx in _PAIR_TABLES:
        rows_ = [jnp.concatenate([t[:, idx[a, kk]] for kk in range(ATT_WIN_ROWS)], axis=-1)
                 for a in range(2)]
        tabs.append(jnp.concatenate(rows_, axis=1))
    return jnp.stack(tabs, axis=1)


def _attn_kernel(q_ref, kp_ref, kc_ref, kn_ref, vp_ref, vc_ref, vn_ref, bias_ref, o_ref,
                 k_scr, v_scr):
    i = pl.program_id(2)
    first = i == 0
    last = i == pl.num_programs(2) - 1
    th = kp_ref.shape[2]
    pq = 2 * GRID_W
    wk = ATT_WIN_ROWS * GRID_W
    for hh in range(q_ref.shape[1]):
        for scr, (p_ref, c_ref, n_ref) in ((k_scr, (kp_ref, kc_ref, kn_ref)),
                                           (v_scr, (vp_ref, vc_ref, vn_ref))):
            scr[hh, 0:th] = p_ref[0, hh]
            scr[hh, th:3 * th] = c_ref[0, hh]
            scr[hh, 3 * th:4 * th] = n_ref[0, hh]

    def pick(tab, m):
        return jnp.where(first, int(tab[0, m]), jnp.where(last, int(tab[2, m]), int(tab[1, m])))

    def scores(hh, m):
        off = pl.multiple_of(pick(_PAIR_STARTS, m) * LANES, LANES)
        q = q_ref[0, hh, m * pq:(m + 1) * pq, :]
        s = lax.dot_general(q, k_scr[hh, pl.ds(off, wk), :], (((1,), (1,)), ((), ())),
                            preferred_element_type=F32)
        return s + bias_ref[hh, pick(_PAIR_TIDS, m)]

    def softmax(s):
        p = jnp.exp(s - jnp.max(s, axis=-1, keepdims=True))
        return p.astype(BF16), jnp.sum(p, axis=-1, keepdims=True)

    def weighted(hh, m, p, l):
        off = pl.multiple_of(pick(_PAIR_STARTS, m) * LANES, LANES)
        o = _dot(p, v_scr[hh, pl.ds(off, wk), :]) / l
        o_ref[0, hh, m * pq:(m + 1) * pq, :] = o.astype(BF16)

    work = [(hh, m) for hh in range(q_ref.shape[1]) for m in range(ATT_PAIRS)]
    s_live, p_live = {}, {}
    for t in range(len(work) + 2):
        if t < len(work):
            s_live[t] = scores(*work[t])
        if 0 <= t - 1 < len(work):
            p_live[t - 1] = softmax(s_live.pop(t - 1))
        if 0 <= t - 2 < len(work):
            weighted(*work[t - 2], *p_live.pop(t - 2))


def _attn_layer(qkv, bias):
    B, _, S, _ = qkv.shape
    H, hb = N_HEADS, ATT_HEADS
    tq = ATT_ROWS * GRID_W
    th = tq // 2
    nblk = S // tq
    assert nblk >= 3 and S % tq == 0
    last_h = S // th - 1
    ng = H // hb

    def cur(part):
        return pl.BlockSpec((1, hb, tq, HEAD_DIM), lambda b_, g, i: (b_, part * ng + g, i, 0))

    def prev(part):
        return pl.BlockSpec((1, hb, th, HEAD_DIM),
                            lambda b_, g, i: (b_, part * ng + g, jnp.maximum(2 * i - 1, 0), 0))

    def nxt(part):
        return pl.BlockSpec((1, hb, th, HEAD_DIM),
                            lambda b_, g, i: (b_, part * ng + g, jnp.minimum(2 * i + 2, last_h), 0))

    nt = bias.shape[1]
    return pl.pallas_call(
        _attn_kernel,
        grid=(B, ng, nblk),
        in_specs=[cur(0), prev(1), cur(1), nxt(1), prev(2), cur(2), nxt(2),
                  pl.BlockSpec((hb, nt, 2 * GRID_W, ATT_WIN_ROWS * GRID_W), lambda b_, g, i: (g, 0, 0, 0))],
        out_specs=pl.BlockSpec((1, hb, tq, HEAD_DIM), lambda b_, g, i: (b_, g, i, 0)),
        out_shape=jax.ShapeDtypeStruct((B, H, S, HEAD_DIM), BF16),
        scratch_shapes=[pltpu.VMEM((hb, ATT_KEY_ROWS * GRID_W, HEAD_DIM), BF16),
                        pltpu.VMEM((hb, ATT_KEY_ROWS * GRID_W, HEAD_DIM), BF16)],
        compiler_params=_params(("parallel", "parallel", "arbitrary")),
        name="natten",
    )(qkv, qkv, qkv, qkv, qkv, qkv, qkv, bias)


def _wo_kernel(o_ref, x_ref, w_ref, y_ref):
    o = jnp.concatenate([o_ref[0, h] for h in range(o_ref.shape[1])], axis=-1)
    y_ref[0] = x_ref[0] + _dot(o, w_ref[...])


def _wo_layer(o, x, w):
    B, S, D = x.shape
    tm = min(WO_TOKENS, S)
    tile = pl.BlockSpec((1, tm, D), lambda b_, i: (b_, i, 0))
    return pl.pallas_call(
        _wo_kernel,
        grid=(B, S // tm),
        in_specs=[pl.BlockSpec((1, N_HEADS, tm, HEAD_DIM), lambda b_, i: (b_, 0, i, 0)),
                  tile, pl.BlockSpec((D, D), lambda b_, i: (0, 0))],
        out_specs=tile,
        out_shape=jax.ShapeDtypeStruct((B, S, D), F32),
        compiler_params=_params(("parallel", "parallel")),
        name="attn_out",
    )(o, x, w)


def _prep_weights(mix_norm, pool_w, pool_b, pool_scale, attn_w_qkv, attn_b_qkv, attn_rpb, attn_w_o,
                  ffn_norm, ffn_w_up, ffn_conv_w, ffn_conv_b, ffn_w_down, final_norm):
    assert pool_w.shape[0] == 1 and attn_w_qkv.shape[0] == 1 and ffn_w_up.shape[0] == 2
    q_scale = jnp.concatenate([jnp.full((D_MODEL,), HEAD_DIM ** -0.5, F32),
                               jnp.ones((2 * D_MODEL,), F32)]).reshape(1, 3 * D_MODEL)
    return dict(
        mix_norm=mix_norm.reshape(-1, 1, D_MODEL),
        pool_w=pool_w[0].astype(BF16),
        pool_b=pool_b[0].reshape(N_POOL_GROUPS, 1, GROUP_DIM),
        pool_scale=pool_scale[0].reshape(1, D_MODEL),
        w_qkv=attn_w_qkv[0].astype(BF16),
        b_qkv=attn_b_qkv[0].reshape(1, 3 * D_MODEL),
        q_scale=q_scale,
        bias=_attn_bias_tables(attn_rpb[0]),
        w_o=attn_w_o[0].astype(BF16),
        ffn=(ffn_norm.reshape(-1, 1, D_MODEL), ffn_w_up.astype(BF16), ffn_conv_w,
             ffn_conv_b.reshape(-1, 1, D_FF), ffn_w_down.astype(BF16)),
        final_norm=final_norm.reshape(1, D_MODEL),
    )


def _trunk(x, p):
    x = _pool_layer(x, p["mix_norm"][0], p["pool_w"], p["pool_b"], p["pool_scale"])
    x = _ffn_layer(x, 0, *p["ffn"], p["final_norm"], final_norm=False)
    qkv = _qkv_layer(x, p["mix_norm"][1], p["w_qkv"], p["b_qkv"], p["q_scale"])
    o = _attn_layer(qkv, p["bias"])
    x = _wo_layer(o, x, p["w_o"])
    return _ffn_layer(x, 1, *p["ffn"], p["final_norm"], final_norm=True)


def kernel(x_prompt, x_sample, mix_norm, pool_w, pool_b, pool_scale, attn_w_qkv, attn_b_qkv, attn_rpb,
           attn_w_o, ffn_norm, ffn_w_up, ffn_conv_w, ffn_conv_b, ffn_w_down, final_norm):
    p = _prep_weights(mix_norm, pool_w, pool_b, pool_scale, attn_w_qkv, attn_b_qkv, attn_rpb, attn_w_o,
                      ffn_norm, ffn_w_up, ffn_conv_w, ffn_conv_b, ffn_w_down, final_norm)
    return _trunk(x_prompt, p), _trunk(x_sample, p)
```

```python
import functools

import numpy as np
import jax
import jax.numpy as jnp
from jax import lax
from jax.experimental import pallas as pl
from jax.experimental.pallas import tpu as pltpu

D_MODEL = 2048
GRID_W = 64
N_POOL_GROUPS = 4
POOL_WINDOWS = (2, 4, 8, 16)
GROUP_DIM = D_MODEL // N_POOL_GROUPS
N_HEADS = 16
HEAD_DIM = D_MODEL // N_HEADS
WIN_H = 8
WIN_W = 16
D_FF = 5632
EPS = 1e-6
NEG_INF = -1e30

LANES = 128
HALO = 8
POOL_TOKENS = 512
FFN_TOKENS = 1024
FFN_CHUNK = 512
QKV_TOKENS = 1024
QKV_CHUNK = 2048
WO_TOKENS = 512
ATT_ROWS = 8
ATT_HEADS = 8
ATT_PAIRS = ATT_ROWS // 2
ATT_WIN_ROWS = WIN_H + 2
ATT_KEY_ROWS = ATT_ROWS + WIN_H
VMEM_LIMIT = 60 * 1024 * 1024

F32 = jnp.float32
BF16 = jnp.bfloat16


def _rms(x, g):
    return x * lax.rsqrt(jnp.mean(x * x, axis=-1, keepdims=True) + EPS) * g


def _dot(a, b):
    return jnp.dot(a, b, preferred_element_type=F32)


def _params(sem):
    return pltpu.CompilerParams(dimension_semantics=sem, vmem_limit_bytes=VMEM_LIMIT)


def _halo_specs(tokens, seq):
    per = tokens // HALO
    last = seq // HALO - 1
    main = pl.BlockSpec((1, tokens, D_MODEL), lambda b, i, *_: (b, i, 0))
    prev = pl.BlockSpec((1, HALO, D_MODEL), lambda b, i, *_: (b, jnp.maximum(i * per - 1, 0), 0))
    nxt = pl.BlockSpec((1, HALO, D_MODEL), lambda b, i, *_: (b, jnp.minimum((i + 1) * per, last), 0))
    return main, prev, nxt


def _pool_kernel(xm_ref, xp_ref, xn_ref, g_ref, w_ref, b_ref, sc_ref, o_ref, *, seq, tp):
    i = pl.program_id(1)
    n = tp + 2 * HALO
    xm = xm_ref[0]
    xa = jnp.concatenate([xp_ref[0], xm, xn_ref[0]], axis=0)
    pos = i * tp - HALO + lax.broadcasted_iota(jnp.int32, (n, 1), 0)
    u = jnp.where((pos >= 0) & (pos < seq), _rms(xa, g_ref[...]), 0.0)
    posm = pos[HALO:HALO + tp]
    for g, w in enumerate(POOL_WINDOWS):
        cols = slice(g * GROUP_DIM, (g + 1) * GROUP_DIM)
        ug = u[:, cols]
        a = ug + pltpu.roll(ug, 1, 0)
        width = 2
        while width < w:
            a = a + pltpu.roll(a, width, 0)
            width *= 2
        s = a if w == 2 else pltpu.roll(a, n - (w // 2 - 1), 0)
        lo = jnp.clip(posm - w // 2, 0, seq)
        hi = jnp.clip(posm - w // 2 + w, 0, seq)
        inv_cnt = 1.0 / (hi - lo).astype(F32)
        d = s[HALO:HALO + tp] * inv_cnt - ug[HALO:HALO + tp]
        y = _dot(d.astype(BF16), w_ref[g]) + b_ref[g]
        o_ref[0, :, cols] = xm[:, cols] + y * sc_ref[:, cols]


def _pool_layer(x, g, w, b, sc):
    B, S, D = x.shape
    tp = min(POOL_TOKENS, S)
    main, prev, nxt = _halo_specs(tp, S)
    const2 = lambda b_, i: (0, 0)
    const3 = lambda b_, i: (0, 0, 0)
    return pl.pallas_call(
        functools.partial(_pool_kernel, seq=S, tp=tp),
        grid=(B, S // tp),
        in_specs=[main, prev, nxt,
                  pl.BlockSpec((1, D), const2),
                  pl.BlockSpec((N_POOL_GROUPS, GROUP_DIM, GROUP_DIM), const3),
                  pl.BlockSpec((N_POOL_GROUPS, 1, GROUP_DIM), const3),
                  pl.BlockSpec((1, D), const2)],
        out_specs=pl.BlockSpec((1, tp, D), lambda b_, i: (b_, i, 0)),
        out_shape=jax.ShapeDtypeStruct((B, S, D), F32),
        compiler_params=_params(("parallel", "parallel")),
        name="pool_layer",
    )(x, x, x, g, w, b, sc)


def _gelu(x):
    return 0.5 * x * (1.0 + lax.erf(x * np.float32(np.sqrt(0.5))))


def _ffn_kernel(xm_ref, xp_ref, xn_ref, nw_ref, wg_ref, wv_ref, cw_ref, cb_ref, wd_ref, fn_ref,
                o_ref, u_ref, *, tm, final_norm):
    i = pl.program_id(1)
    j = pl.program_id(2)
    n = tm + 2 * HALO

    @pl.when(j == 0)
    def _():
        xm = xm_ref[0]
        g = nw_ref[0]
        u_ref[0:tm] = _rms(xm, g).astype(BF16)
        un = jnp.where(i < pl.num_programs(1) - 1, _rms(xn_ref[0], g), 0.0)
        up = jnp.where(i > 0, _rms(xp_ref[0], g), 0.0)
        u_ref[tm:n] = jnp.concatenate([un, up], axis=0).astype(BF16)
        o_ref[0] = xm

    gate = _dot(u_ref[...], wg_ref[0])
    val = _dot(u_ref[0:tm], wv_ref[0])
    cw = cw_ref[0]
    conv = (pltpu.roll(gate, 1, 0)[:tm] * cw[0:1] + gate[:tm] * cw[1:2]
            + pltpu.roll(gate, n - 1, 0)[:tm] * cw[2:3] + cb_ref[0])
    h = _gelu(conv) * val
    o_ref[0] += _dot(h.astype(BF16), wd_ref[0])

    if final_norm:
        @pl.when(j == pl.num_programs(2) - 1)
        def _():
            o_ref[0] = _rms(o_ref[0], fn_ref[...])


def _ffn_layer(x, layer, nw, w_up, cw, cb, w_down, fn, *, final_norm):
    B, S, D = x.shape
    tm = min(FFN_TOKENS, S)
    tf = FFN_CHUNK
    nj = D_FF // tf
    main, prev, nxt = _halo_specs(tm, S)
    return pl.pallas_call(
        functools.partial(_ffn_kernel, tm=tm, final_norm=final_norm),
        grid=(B, S // tm, nj),
        in_specs=[main, prev, nxt,
                  pl.BlockSpec((1, 1, D), lambda b_, i, j: (layer, 0, 0)),
                  pl.BlockSpec((1, D, tf), lambda b_, i, j: (layer, 0, j)),
                  pl.BlockSpec((1, D, tf), lambda b_, i, j: (layer, 0, nj + j)),
                  pl.BlockSpec((1, 3, tf), lambda b_, i, j: (layer, 0, j)),
                  pl.BlockSpec((1, 1, tf), lambda b_, i, j: (layer, 0, j)),
                  pl.BlockSpec((1, tf, D), lambda b_, i, j: (layer, j, 0)),
                  pl.BlockSpec((1, D), lambda b_, i, j: (0, 0))],
        out_specs=pl.BlockSpec((1, tm, D), lambda b_, i, j: (b_, i, 0)),
        out_shape=jax.ShapeDtypeStruct((B, S, D), F32),
        scratch_shapes=[pltpu.VMEM((tm + 2 * HALO, D), BF16)],
        compiler_params=_params(("parallel", "parallel", "arbitrary")),
        name="ffn_final" if final_norm else "ffn",
    )(x, x, x, nw, w_up, w_up, cw, cb, w_down, fn)


def _qkv_kernel(x_ref, nw_ref, w_ref, b_ref, s_ref, o_ref, u_ref):
    @pl.when(pl.program_id(2) == 0)
    def _():
        u_ref[...] = _rms(x_ref[0], nw_ref[...]).astype(BF16)

    res = ((_dot(u_ref[...], w_ref[...]) + b_ref[...]) * s_ref[...]).astype(BF16)
    for hh in range(o_ref.shape[1]):
        o_ref[0, hh] = res[:, hh * HEAD_DIM:(hh + 1) * HEAD_DIM]


def _qkv_layer(x, nw, w, b, s):
    B, S, D = x.shape
    tm = min(QKV_TOKENS, S)
    tn = QKV_CHUNK
    N = w.shape[1]
    return pl.pallas_call(
        _qkv_kernel,
        grid=(B, S // tm, N // tn),
        in_specs=[pl.BlockSpec((1, tm, D), lambda b_, i, j: (b_, i, 0)),
                  pl.BlockSpec((1, D), lambda b_, i, j: (0, 0)),
                  pl.BlockSpec((D, tn), lambda b_, i, j: (0, j)),
                  pl.BlockSpec((1, tn), lambda b_, i, j: (0, j)),
                  pl.BlockSpec((1, tn), lambda b_, i, j: (0, j))],
        out_specs=pl.BlockSpec((1, tn // HEAD_DIM, tm, HEAD_DIM), lambda b_, i, j: (b_, j, i, 0)),
        out_shape=jax.ShapeDtypeStruct((B, N // HEAD_DIM, S, HEAD_DIM), BF16),
        scratch_shapes=[pltpu.VMEM((tm, D), BF16)],
        compiler_params=_params(("parallel", "parallel", "arbitrary")),
        name="qkv",
    )(x, nw, w, b, s)


def _pair_patterns():
    rows = 4 * ATT_ROWS
    masked = 2 * WIN_H - 1
    tables, starts, tids = {}, np.zeros((3, ATT_PAIRS), int), np.zeros((3, ATT_PAIRS), int)
    for v, r0 in enumerate((0, ATT_ROWS, rows - ATT_ROWS)):
        kb = np.clip(r0 - WIN_H // 2, 0, rows - ATT_KEY_ROWS)
        for m in range(ATT_PAIRS):
            r = r0 + 2 * m + np.arange(2)
            kstart = np.clip(r - WIN_H // 2, 0, rows - WIN_H) - kb
            w0 = min(kstart[0] // 2, (ATT_KEY_ROWS - ATT_WIN_ROWS) // 2)
            kk = 2 * w0 + np.arange(ATT_WIN_ROWS)
            ok = (kk[None, :] >= kstart[:, None]) & (kk[None, :] < kstart[:, None] + WIN_H)
            assert ok.sum() == 2 * WIN_H
            dr = (kb + kk)[None, :] - r[:, None] + WIN_H - 1
            idx = np.where(ok, dr, masked)
            assert idx.min() >= 0 and idx.max() <= masked
            starts[v, m] = w0
            tids[v, m] = tables.setdefault(tuple(idx.ravel()), len(tables))
    return [np.array(t).reshape(2, ATT_WIN_ROWS) for t in tables], starts, tids


_PAIR_TABLES, _PAIR_STARTS, _PAIR_TIDS = _pair_patterns()


def _attn_bias_tables(rpb):
    H = rpb.shape[0]
    c = np.arange(GRID_W)
    cstart = np.clip(c - WIN_W // 2, 0, GRID_W - WIN_W)
    col_ok = (c[None, :] >= cstart[:, None]) & (c[None, :] < cstart[:, None] + WIN_W)
    span = 2 * GRID_W
    vi = np.clip(np.arange(span) - (GRID_W - 1), -(WIN_W - 1), WIN_W - 1) + WIN_W - 1
    vp = rpb[:, :, vi]
    skew = jnp.broadcast_to(vp[:, :, None, :], (H, 2 * WIN_H - 1, GRID_W, span))
    skew = skew.reshape(H, 2 * WIN_H - 1, GRID_W * span)[:, :, :GRID_W * (span - 1)]
    t = skew.reshape(H, 2 * WIN_H - 1, GRID_W, span - 1)[..., GRID_W - 1:]
    t = jnp.where(col_ok, t, NEG_INF)
    t = jnp.concatenate([t, jnp.full((H, 1, GRID_W, GRID_W), NEG_INF, F32)], axis=1)
    tabs = []
    for idx in _PAIR_TABLES:
        rows_ = [jnp.concatenate([t[:, idx[a, kk]] for kk in range(ATT_WIN_ROWS)], axis=-1)
                 for a in range(2)]
        tabs.append(jnp.concatenate(rows_, axis=1))
    return jnp.stack(tabs, axis=1)


def _attn_kernel(q_ref, k_ref, v_ref, bias_ref, o_ref):
    i = pl.program_id(2)
    first = i == 0
    last = i == pl.num_programs(2) - 1
    pq = 2 * GRID_W
    wk = ATT_WIN_ROWS * GRID_W

    def pick(tab, m):
        return jnp.where(first, int(tab[0, m]), jnp.where(last, int(tab[2, m]), int(tab[1, m])))

    def scores(hh, m):
        off = pl.multiple_of(pick(_PAIR_STARTS, m) * LANES, LANES)
        q = q_ref[0, hh, m * pq:(m + 1) * pq, :]
        s = lax.dot_general(q, k_ref[0, hh, pl.ds(off, wk), :], (((1,), (1,)), ((), ())),
                            preferred_element_type=F32)
        return s + bias_ref[hh, pick(_PAIR_TIDS, m)]

    def softmax(s):
        p = jnp.exp(s - jnp.max(s, axis=-1, keepdims=True))
        return p.astype(BF16), jnp.sum(p, axis=-1, keepdims=True)

    def weighted(hh, m, p, l):
        off = pl.multiple_of(pick(_PAIR_STARTS, m) * LANES, LANES)
        o = _dot(p, v_ref[0, hh, pl.ds(off, wk), :]) / l
        o_ref[0, hh, m * pq:(m + 1) * pq, :] = o.astype(BF16)

    work = [(hh, m) for hh in range(q_ref.shape[1]) for m in range(ATT_PAIRS)]
    s_live, p_live = {}, {}
    for t in range(len(work) + 2):
        if t < len(work):
            s_live[t] = scores(*work[t])
        if 0 <= t - 1 < len(work):
            p_live[t - 1] = softmax(s_live.pop(t - 1))
        if 0 <= t - 2 < len(work):
            weighted(*work[t - 2], *p_live.pop(t - 2))


def _attn_layer(qkv, bias):
    B, _, S, _ = qkv.shape
    H, hb = N_HEADS, ATT_HEADS
    tq = ATT_ROWS * GRID_W
    tk = ATT_KEY_ROWS * GRID_W
    halo = (tk - tq) // 2
    nblk = S // tq
    assert nblk >= 3 and S % tq == 0
    ng = H // hb

    def keys(part):
        return pl.BlockSpec(
            (pl.Element(1), pl.Element(hb), pl.Element(tk), pl.Element(HEAD_DIM)),
            lambda b_, g, i: (b_, (part * ng + g) * hb, halo * jnp.clip(2 * i - 1, 0, (S - tk) // halo), 0))

    nt = bias.shape[1]
    return pl.pallas_call(
        _attn_kernel,
        grid=(B, ng, nblk),
        in_specs=[pl.BlockSpec((1, hb, tq, HEAD_DIM), lambda b_, g, i: (b_, g, i, 0)),
                  keys(1), keys(2),
                  pl.BlockSpec((hb, nt, 2 * GRID_W, ATT_WIN_ROWS * GRID_W), lambda b_, g, i: (g, 0, 0, 0))],
        out_specs=pl.BlockSpec((1, hb, tq, HEAD_DIM), lambda b_, g, i: (b_, g, i, 0)),
        out_shape=jax.ShapeDtypeStruct((B, H, S, HEAD_DIM), BF16),
        compiler_params=_params(("parallel", "parallel", "arbitrary")),
        name="natten",
    )(qkv, qkv, qkv, bias)


def _wo_kernel(o_ref, x_ref, w_ref, y_ref):
    o = jnp.concatenate([o_ref[0, h] for h in range(o_ref.shape[1])], axis=-1)
    y_ref[0] = x_ref[0] + _dot(o, w_ref[...])


def _wo_layer(o, x, w):
    B, S, D = x.shape
    tm = min(WO_TOKENS, S)
    tile = pl.BlockSpec((1, tm, D), lambda b_, i: (b_, i, 0))
    return pl.pallas_call(
        _wo_kernel,
        grid=(B, S // tm),
        in_specs=[pl.BlockSpec((1, N_HEADS, tm, HEAD_DIM), lambda b_, i: (b_, 0, i, 0)),
                  tile, pl.BlockSpec((D, D), lambda b_, i: (0, 0))],
        out_specs=tile,
        out_shape=jax.ShapeDtypeStruct((B, S, D), F32),
        compiler_params=_params(("parallel", "parallel")),
        name="attn_out",
    )(o, x, w)


def _prep_weights(mix_norm, pool_w, pool_b, pool_scale, attn_w_qkv, attn_b_qkv, attn_rpb, attn_w_o,
                  ffn_norm, ffn_w_up, ffn_conv_w, ffn_conv_b, ffn_w_down, final_norm):
    assert pool_w.shape[0] == 1 and attn_w_qkv.shape[0] == 1 and ffn_w_up.shape[0] == 2
    q_scale = jnp.concatenate([jnp.full((D_MODEL,), HEAD_DIM ** -0.5, F32),
                               jnp.ones((2 * D_MODEL,), F32)]).reshape(1, 3 * D_MODEL)
    return dict(
        mix_norm=mix_norm.reshape(-1, 1, D_MODEL),
        pool_w=pool_w[0].astype(BF16),
        pool_b=pool_b[0].reshape(N_POOL_GROUPS, 1, GROUP_DIM),
        pool_scale=pool_scale[0].reshape(1, D_MODEL),
        w_qkv=attn_w_qkv[0].astype(BF16),
        b_qkv=attn_b_qkv[0].reshape(1, 3 * D_MODEL),
        q_scale=q_scale,
        bias=_attn_bias_tables(attn_rpb[0]),
        w_o=attn_w_o[0].astype(BF16),
        ffn=(ffn_norm.reshape(-1, 1, D_MODEL), ffn_w_up.astype(BF16), ffn_conv_w,
             ffn_conv_b.reshape(-1, 1, D_FF), ffn_w_down.astype(BF16)),
        final_norm=final_norm.reshape(1, D_MODEL),
    )


def _trunk(x, p):
    x = _pool_layer(x, p["mix_norm"][0], p["pool_w"], p["pool_b"], p["pool_scale"])
    x = _ffn_layer(x, 0, *p["ffn"], p["final_norm"], final_norm=False)
    qkv = _qkv_layer(x, p["mix_norm"][1], p["w_qkv"], p["b_qkv"], p["q_scale"])
    o = _attn_layer(qkv, p["bias"])
    x = _wo_layer(o, x, p["w_o"])
    return _ffn_layer(x, 1, *p["ffn"], p["final_norm"], final_norm=True)


def kernel(x_prompt, x_sample, mix_norm, pool_w, pool_b, pool_scale, attn_w_qkv, attn_b_qkv, attn_rpb,
           attn_w_o, ffn_norm, ffn_w_up, ffn_conv_w, ffn_conv_b, ffn_w_down, final_norm):
    p = _prep_weights(mix_norm, pool_w, pool_b, pool_scale, attn_w_qkv, attn_b_qkv, attn_rpb, attn_w_o,
                      ffn_norm, ffn_w_up, ffn_conv_w, ffn_conv_b, ffn_w_down, final_norm)
    return _trunk(x_prompt, p), _trunk(x_sample, p)
```

```python
import functools

import numpy as np
import jax
import jax.numpy as jnp
from jax import lax
from jax.experimental import pallas as pl
from jax.experimental.pallas import tpu as pltpu

D_MODEL = 2048
GRID_W = 64
N_POOL_GROUPS = 4
POOL_WINDOWS = (2, 4, 8, 16)
GROUP_DIM = D_MODEL // N_POOL_GROUPS
N_HEADS = 16
HEAD_DIM = D_MODEL // N_HEADS
WIN_H = 8
WIN_W = 16
D_FF = 5632
EPS = 1e-6
NEG_INF = -1e30
LOG2_E = float(np.log2(np.e))

HALO = 8
POOL_TOKENS = 512
FFN_TOKENS = 1024
FFN_CHUNK = 512
QKV_TOKENS = 1024
QKV_CHUNK = 2048
WO_TOKENS = 512
ATT_ROWS = 8
ATT_HEADS = 8
ATT_KEY_ROWS = ATT_ROWS + WIN_H
ATT_SKEW = 4
VMEM_LIMIT = 60 * 1024 * 1024

F32 = jnp.float32
BF16 = jnp.bfloat16


def _rms(x, g):
    return x * lax.rsqrt(jnp.mean(x * x, axis=-1, keepdims=True) + EPS) * g


def _dot(a, b):
    return jnp.dot(a, b, preferred_element_type=F32)


def _params(sem):
    return pltpu.CompilerParams(dimension_semantics=sem, vmem_limit_bytes=VMEM_LIMIT)


def _halo_specs(tokens, seq):
    per = tokens // HALO
    last = seq // HALO - 1
    main = pl.BlockSpec((1, tokens, D_MODEL), lambda b, i, *_: (b, i, 0))
    prev = pl.BlockSpec((1, HALO, D_MODEL), lambda b, i, *_: (b, jnp.maximum(i * per - 1, 0), 0))
    nxt = pl.BlockSpec((1, HALO, D_MODEL), lambda b, i, *_: (b, jnp.minimum((i + 1) * per, last), 0))
    return main, prev, nxt


def _pool_kernel(xm_ref, xp_ref, xn_ref, g_ref, w_ref, b_ref, sc_ref, o_ref, *, seq, tp):
    i = pl.program_id(1)
    n = tp + 2 * HALO
    xm = xm_ref[0]
    xa = jnp.concatenate([xp_ref[0], xm, xn_ref[0]], axis=0)
    pos = i * tp - HALO + lax.broadcasted_iota(jnp.int32, (n, 1), 0)
    u = jnp.where((pos >= 0) & (pos < seq), _rms(xa, g_ref[...]), 0.0)
    posm = pos[HALO:HALO + tp]
    for g, w in enumerate(POOL_WINDOWS):
        cols = slice(g * GROUP_DIM, (g + 1) * GROUP_DIM)
        ug = u[:, cols]
        a = ug + pltpu.roll(ug, 1, 0)
        width = 2
        while width < w:
            a = a + pltpu.roll(a, width, 0)
            width *= 2
        s = a if w == 2 else pltpu.roll(a, n - (w // 2 - 1), 0)
        lo = jnp.clip(posm - w // 2, 0, seq)
        hi = jnp.clip(posm - w // 2 + w, 0, seq)
        inv_cnt = 1.0 / (hi - lo).astype(F32)
        d = s[HALO:HALO + tp] * inv_cnt - ug[HALO:HALO + tp]
        y = _dot(d.astype(BF16), w_ref[g]) + b_ref[g]
        o_ref[0, :, cols] = xm[:, cols] + y * sc_ref[:, cols]


def _pool_layer(x, g, w, b, sc):
    B, S, D = x.shape
    tp = min(POOL_TOKENS, S)
    main, prev, nxt = _halo_specs(tp, S)
    const2 = lambda b_, i: (0, 0)
    const3 = lambda b_, i: (0, 0, 0)
    return pl.pallas_call(
        functools.partial(_pool_kernel, seq=S, tp=tp),
        grid=(B, S // tp),
        in_specs=[main, prev, nxt,
                  pl.BlockSpec((1, D), const2),
                  pl.BlockSpec((N_POOL_GROUPS, GROUP_DIM, GROUP_DIM), const3),
                  pl.BlockSpec((N_POOL_GROUPS, 1, GROUP_DIM), const3),
                  pl.BlockSpec((1, D), const2)],
        out_specs=pl.BlockSpec((1, tp, D), lambda b_, i: (b_, i, 0)),
        out_shape=jax.ShapeDtypeStruct((B, S, D), F32),
        compiler_params=_params(("parallel", "parallel")),
        name="pool_layer",
    )(x, x, x, g, w, b, sc)


def _gelu(x):
    return 0.5 * x * (1.0 + lax.erf(x * np.float32(np.sqrt(0.5))))


def _ffn_kernel(xm_ref, xp_ref, xn_ref, nw_ref, wg_ref, wv_ref, cw_ref, cb_ref, wd_ref, fn_ref,
                o_ref, u_ref, *, tm, final_norm):
    i = pl.program_id(1)
    j = pl.program_id(2)
    n = tm + 2 * HALO

    @pl.when(j == 0)
    def _():
        xm = xm_ref[0]
        g = nw_ref[0]
        u_ref[0:tm] = _rms(xm, g).astype(BF16)
        un = jnp.where(i < pl.num_programs(1) - 1, _rms(xn_ref[0], g), 0.0)
        up = jnp.where(i > 0, _rms(xp_ref[0], g), 0.0)
        u_ref[tm:n] = jnp.concatenate([un, up], axis=0).astype(BF16)
        o_ref[0] = xm

    gate = _dot(u_ref[...], wg_ref[0])
    val = _dot(u_ref[0:tm], wv_ref[0])
    cw = cw_ref[0]
    conv = (pltpu.roll(gate, 1, 0)[:tm] * cw[0:1] + gate[:tm] * cw[1:2]
            + pltpu.roll(gate, n - 1, 0)[:tm] * cw[2:3] + cb_ref[0])
    h = _gelu(conv) * val
    o_ref[0] += _dot(h.astype(BF16), wd_ref[0])

    if final_norm:
        @pl.when(j == pl.num_programs(2) - 1)
        def _():
            o_ref[0] = _rms(o_ref[0], fn_ref[...])


def _ffn_layer(x, layer, nw, w_up, cw, cb, w_down, fn, *, final_norm):
    B, S, D = x.shape
    tm = min(FFN_TOKENS, S)
    tf = FFN_CHUNK
    nj = D_FF // tf
    main, prev, nxt = _halo_specs(tm, S)
    return pl.pallas_call(
        functools.partial(_ffn_kernel, tm=tm, final_norm=final_norm),
        grid=(B, S // tm, nj),
        in_specs=[main, prev, nxt,
                  pl.BlockSpec((1, 1, D), lambda b_, i, j: (layer, 0, 0)),
                  pl.BlockSpec((1, D, tf), lambda b_, i, j: (layer, 0, j)),
                  pl.BlockSpec((1, D, tf), lambda b_, i, j: (layer, 0, nj + j)),
                  pl.BlockSpec((1, 3, tf), lambda b_, i, j: (layer, 0, j)),
                  pl.BlockSpec((1, 1, tf), lambda b_, i, j: (layer, 0, j)),
                  pl.BlockSpec((1, tf, D), lambda b_, i, j: (layer, j, 0)),
                  pl.BlockSpec((1, D), lambda b_, i, j: (0, 0))],
        out_specs=pl.BlockSpec((1, tm, D), lambda b_, i, j: (b_, i, 0)),
        out_shape=jax.ShapeDtypeStruct((B, S, D), F32),
        scratch_shapes=[pltpu.VMEM((tm + 2 * HALO, D), BF16)],
        compiler_params=_params(("parallel", "parallel", "arbitrary")),
        name="ffn_final" if final_norm else "ffn",
    )(x, x, x, nw, w_up, w_up, cw, cb, w_down, fn)


def _qkv_kernel(x_ref, nw_ref, w_ref, b_ref, s_ref, o_ref, u_ref):
    @pl.when(pl.program_id(2) == 0)
    def _():
        u_ref[...] = _rms(x_ref[0], nw_ref[...]).astype(BF16)

    res = ((_dot(u_ref[...], w_ref[...]) + b_ref[...]) * s_ref[...]).astype(BF16)
    for hh in range(o_ref.shape[1]):
        o_ref[0, hh] = res[:, hh * HEAD_DIM:(hh + 1) * HEAD_DIM]


def _qkv_layer(x, nw, w, b, s):
    B, S, D = x.shape
    tm = min(QKV_TOKENS, S)
    tn = QKV_CHUNK
    N = w.shape[1]
    return pl.pallas_call(
        _qkv_kernel,
        grid=(B, S // tm, N // tn),
        in_specs=[pl.BlockSpec((1, tm, D), lambda b_, i, j: (b_, i, 0)),
                  pl.BlockSpec((1, D), lambda b_, i, j: (0, 0)),
                  pl.BlockSpec((D, tn), lambda b_, i, j: (0, j)),
                  pl.BlockSpec((1, tn), lambda b_, i, j: (0, j)),
                  pl.BlockSpec((1, tn), lambda b_, i, j: (0, j))],
        out_specs=pl.BlockSpec((1, tn // HEAD_DIM, tm, HEAD_DIM), lambda b_, i, j: (b_, j, i, 0)),
        out_shape=jax.ShapeDtypeStruct((B, N // HEAD_DIM, S, HEAD_DIM), BF16),
        scratch_shapes=[pltpu.VMEM((tm, D), BF16)],
        compiler_params=_params(("parallel", "parallel", "arbitrary")),
        name="qkv",
    )(x, nw, w, b, s)


def _row_patterns():
    rows = 4 * ATT_ROWS
    kstart, tid = np.zeros((3, ATT_ROWS), int), np.zeros((3, ATT_ROWS), int)
    for v, r0 in enumerate((0, ATT_ROWS, rows - ATT_ROWS)):
        kb = np.clip(r0 - WIN_H // 2, 0, rows - ATT_KEY_ROWS)
        r = r0 + np.arange(ATT_ROWS)
        rs = np.clip(r - WIN_H // 2, 0, rows - WIN_H)
        kstart[v] = rs - kb
        tid[v] = rs - r + WIN_H - 1
    assert kstart.min() >= 0 and kstart.max() + WIN_H <= ATT_KEY_ROWS
    assert tid.min() >= 0 and tid.max() < WIN_H
    return kstart, tid


_ROW_KSTART, _ROW_TID = _row_patterns()


def _attn_bias_tables(rpb):
    H = rpb.shape[0]
    c = np.arange(GRID_W)
    cstart = np.clip(c - WIN_W // 2, 0, GRID_W - WIN_W)
    col_ok = (c[None, :] >= cstart[:, None]) & (c[None, :] < cstart[:, None] + WIN_W)
    span = 2 * GRID_W
    vi = np.clip(np.arange(span) - (GRID_W - 1), -(WIN_W - 1), WIN_W - 1) + WIN_W - 1
    vp = rpb[:, :, vi]
    skew = jnp.broadcast_to(vp[:, :, None, :], (H, 2 * WIN_H - 1, GRID_W, span))
    skew = skew.reshape(H, 2 * WIN_H - 1, GRID_W * span)[:, :, :GRID_W * (span - 1)]
    t = skew.reshape(H, 2 * WIN_H - 1, GRID_W, span - 1)[..., GRID_W - 1:]
    t = jnp.where(col_ok, t * LOG2_E, NEG_INF)
    tabs = [jnp.concatenate([t[:, s + kk] for kk in range(WIN_H)], axis=-1) for s in range(WIN_H)]
    return jnp.stack(tabs, axis=1)


def _attn_kernel(q_ref, k_ref, v_ref, bias_ref, o_ref):
    i = pl.program_id(2)
    first = i == 0
    last = i == pl.num_programs(2) - 1
    nq = GRID_W
    nk = WIN_H * GRID_W

    def pick(tab, r):
        return jnp.where(first, int(tab[0, r]), jnp.where(last, int(tab[2, r]), int(tab[1, r])))

    def scores(hh, r):
        off = pl.multiple_of(pick(_ROW_KSTART, r) * GRID_W, GRID_W)
        q = q_ref[0, hh, r * nq:(r + 1) * nq, :]
        s = lax.dot_general(q, k_ref[0, hh, pl.ds(off, nk), :], (((1,), (1,)), ((), ())),
                            preferred_element_type=F32)
        return s + bias_ref[hh, pick(_ROW_TID, r)]

    def softmax(s):
        p = jnp.exp2(s - jnp.max(s, axis=-1, keepdims=True))
        return p.astype(BF16), jnp.sum(p, axis=-1, keepdims=True)

    def weighted(hh, r, p, l):
        off = pl.multiple_of(pick(_ROW_KSTART, r) * GRID_W, GRID_W)
        o = _dot(p, v_ref[0, hh, pl.ds(off, nk), :]) / l
        o_ref[0, hh, r * nq:(r + 1) * nq, :] = o.astype(BF16)

    work = [(hh, r) for hh in range(q_ref.shape[1]) for r in range(ATT_ROWS)]
    s_live, p_live = {}, {}
    d1, d2 = ATT_SKEW, 2 * ATT_SKEW
    for t in range(len(work) + d2):
        if t < len(work):
            s_live[t] = scores(*work[t])
        if 0 <= t - d1 < len(work):
            p_live[t - d1] = softmax(s_live.pop(t - d1))
        if 0 <= t - d2 < len(work):
            weighted(*work[t - d2], *p_live.pop(t - d2))


def _attn_layer(qkv, bias):
    B, _, S, _ = qkv.shape
    H, hb = N_HEADS, ATT_HEADS
    tq = ATT_ROWS * GRID_W
    tk = ATT_KEY_ROWS * GRID_W
    halo = (tk - tq) // 2
    nblk = S // tq
    assert nblk >= 3 and S % tq == 0
    ng = H // hb

    def keys(part):
        return pl.BlockSpec(
            (pl.Element(1), pl.Element(hb), pl.Element(tk), pl.Element(HEAD_DIM)),
            lambda b_, g, i: (b_, (part * ng + g) * hb, halo * jnp.clip(2 * i - 1, 0, (S - tk) // halo), 0))

    nt = bias.shape[1]
    return pl.pallas_call(
        _attn_kernel,
        grid=(B, ng, nblk),
        in_specs=[pl.BlockSpec((1, hb, tq, HEAD_DIM), lambda b_, g, i: (b_, g, i, 0)),
                  keys(1), keys(2),
                  pl.BlockSpec((hb, nt, GRID_W, WIN_H * GRID_W), lambda b_, g, i: (g, 0, 0, 0))],
        out_specs=pl.BlockSpec((1, hb, tq, HEAD_DIM), lambda b_, g, i: (b_, g, i, 0)),
        out_shape=jax.ShapeDtypeStruct((B, H, S, HEAD_DIM), BF16),
        compiler_params=_params(("parallel", "parallel", "arbitrary")),
        name="natten",
    )(qkv, qkv, qkv, bias)


def _wo_kernel(o_ref, x_ref, w_ref, y_ref):
    o = jnp.concatenate([o_ref[0, h] for h in range(o_ref.shape[1])], axis=-1)
    y_ref[0] = x_ref[0] + _dot(o, w_ref[...])


def _wo_layer(o, x, w):
    B, S, D = x.shape
    tm = min(WO_TOKENS, S)
    tile = pl.BlockSpec((1, tm, D), lambda b_, i: (b_, i, 0))
    return pl.pallas_call(
        _wo_kernel,
        grid=(B, S // tm),
        in_specs=[pl.BlockSpec((1, N_HEADS, tm, HEAD_DIM), lambda b_, i: (b_, 0, i, 0)),
                  tile, pl.BlockSpec((D, D), lambda b_, i: (0, 0))],
        out_specs=tile,
        out_shape=jax.ShapeDtypeStruct((B, S, D), F32),
        compiler_params=_params(("parallel", "parallel")),
        name="attn_out",
    )(o, x, w)


def _prep_weights(mix_norm, pool_w, pool_b, pool_scale, attn_w_qkv, attn_b_qkv, attn_rpb, attn_w_o,
                  ffn_norm, ffn_w_up, ffn_conv_w, ffn_conv_b, ffn_w_down, final_norm):
    assert pool_w.shape[0] == 1 and attn_w_qkv.shape[0] == 1 and ffn_w_up.shape[0] == 2
    q_scale = jnp.concatenate([jnp.full((D_MODEL,), HEAD_DIM ** -0.5 * LOG2_E, F32),
                               jnp.ones((2 * D_MODEL,), F32)]).reshape(1, 3 * D_MODEL)
    return dict(
        mix_norm=mix_norm.reshape(-1, 1, D_MODEL),
        pool_w=pool_w[0].astype(BF16),
        pool_b=pool_b[0].reshape(N_POOL_GROUPS, 1, GROUP_DIM),
        pool_scale=pool_scale[0].reshape(1, D_MODEL),
        w_qkv=attn_w_qkv[0].astype(BF16),
        b_qkv=attn_b_qkv[0].reshape(1, 3 * D_MODEL),
        q_scale=q_scale,
        bias=_attn_bias_tables(attn_rpb[0]),
        w_o=attn_w_o[0].astype(BF16),
        ffn=(ffn_norm.reshape(-1, 1, D_MODEL), ffn_w_up.astype(BF16), ffn_conv_w,
             ffn_conv_b.reshape(-1, 1, D_FF), ffn_w_down.astype(BF16)),
        final_norm=final_norm.reshape(1, D_MODEL),
    )


def _trunk(x, p):
    x = _pool_layer(x, p["mix_norm"][0], p["pool_w"], p["pool_b"], p["pool_scale"])
    x = _ffn_layer(x, 0, *p["ffn"], p["final_norm"], final_norm=False)
    qkv = _qkv_layer(x, p["mix_norm"][1], p["w_qkv"], p["b_qkv"], p["q_scale"])
    o = _attn_layer(qkv, p["bias"])
    x = _wo_layer(o, x, p["w_o"])
    return _ffn_layer(x, 1, *p["ffn"], p["final_norm"], final_norm=True)


def kernel(x_prompt, x_sample, mix_norm, pool_w, pool_b, pool_scale, attn_w_qkv, attn_b_qkv, attn_rpb,
           attn_w_o, ffn_norm, ffn_w_up, ffn_conv_w, ffn_conv_b, ffn_w_down, final_norm):
    p = _prep_weights(mix_norm, pool_w, pool_b, pool_scale, attn_w_qkv, attn_b_qkv, attn_rpb, attn_w_o,
                      ffn_norm, ffn_w_up, ffn_conv_w, ffn_conv_b, ffn_w_down, final_norm)
    return _trunk(x_prompt, p), _trunk(x_sample, p)
```

```python
import functools

import numpy as np
import jax
import jax.numpy as jnp
from jax import lax
from jax.experimental import pallas as pl
from jax.experimental.pallas import tpu as pltpu

D_MODEL = 2048
GRID_W = 64
N_POOL_GROUPS = 4
POOL_WINDOWS = (2, 4, 8, 16)
GROUP_DIM = D_MODEL // N_POOL_GROUPS
N_HEADS = 16
HEAD_DIM = D_MODEL // N_HEADS
WIN_H = 8
WIN_W = 16
D_FF = 5632
EPS = 1e-6
NEG_INF = -1e30
LOG2_E = float(np.log2(np.e))

HALO = 8
POOL_TOKENS = 512
FFN_TOKENS = 1024
FFN_CHUNK = 512
QKV_TOKENS = 1024
QKV_CHUNK = 2048
WO_TOKENS = 512
ATT_ROWS = 32
ATT_HEADS = 2
ATT_KEY_ROWS = ATT_ROWS + WIN_H
ATT_SKEW = 4
VMEM_LIMIT = 60 * 1024 * 1024

F32 = jnp.float32
BF16 = jnp.bfloat16


def _rms(x, g):
    return x * lax.rsqrt(jnp.mean(x * x, axis=-1, keepdims=True) + EPS) * g


def _dot(a, b):
    return jnp.dot(a, b, preferred_element_type=F32)


def _params(sem):
    return pltpu.CompilerParams(dimension_semantics=sem, vmem_limit_bytes=VMEM_LIMIT)


def _halo_specs(tokens, seq):
    per = tokens // HALO
    last = seq // HALO - 1
    main = pl.BlockSpec((1, tokens, D_MODEL), lambda b, i, *_: (b, i, 0))
    prev = pl.BlockSpec((1, HALO, D_MODEL), lambda b, i, *_: (b, jnp.maximum(i * per - 1, 0), 0))
    nxt = pl.BlockSpec((1, HALO, D_MODEL), lambda b, i, *_: (b, jnp.minimum((i + 1) * per, last), 0))
    return main, prev, nxt


def _pool_kernel(xm_ref, xp_ref, xn_ref, g_ref, w_ref, b_ref, sc_ref, o_ref, *, seq, tp):
    i = pl.program_id(1)
    n = tp + 2 * HALO
    xm = xm_ref[0]
    xa = jnp.concatenate([xp_ref[0], xm, xn_ref[0]], axis=0)
    pos = i * tp - HALO + lax.broadcasted_iota(jnp.int32, (n, 1), 0)
    u = jnp.where((pos >= 0) & (pos < seq), _rms(xa, g_ref[...]), 0.0)
    posm = pos[HALO:HALO + tp]
    for g, w in enumerate(POOL_WINDOWS):
        cols = slice(g * GROUP_DIM, (g + 1) * GROUP_DIM)
        ug = u[:, cols]
        a = ug + pltpu.roll(ug, 1, 0)
        width = 2
        while width < w:
            a = a + pltpu.roll(a, width, 0)
            width *= 2
        s = a if w == 2 else pltpu.roll(a, n - (w // 2 - 1), 0)
        lo = jnp.clip(posm - w // 2, 0, seq)
        hi = jnp.clip(posm - w // 2 + w, 0, seq)
        inv_cnt = 1.0 / (hi - lo).astype(F32)
        d = s[HALO:HALO + tp] * inv_cnt - ug[HALO:HALO + tp]
        y = _dot(d.astype(BF16), w_ref[g]) + b_ref[g]
        o_ref[0, :, cols] = xm[:, cols] + y * sc_ref[:, cols]


def _pool_layer(x, g, w, b, sc):
    B, S, D = x.shape
    tp = min(POOL_TOKENS, S)
    main, prev, nxt = _halo_specs(tp, S)
    const2 = lambda b_, i: (0, 0)
    const3 = lambda b_, i: (0, 0, 0)
    return pl.pallas_call(
        functools.partial(_pool_kernel, seq=S, tp=tp),
        grid=(B, S // tp),
        in_specs=[main, prev, nxt,
                  pl.BlockSpec((1, D), const2),
                  pl.BlockSpec((N_POOL_GROUPS, GROUP_DIM, GROUP_DIM), const3),
                  pl.BlockSpec((N_POOL_GROUPS, 1, GROUP_DIM), const3),
                  pl.BlockSpec((1, D), const2)],
        out_specs=pl.BlockSpec((1, tp, D), lambda b_, i: (b_, i, 0)),
        out_shape=jax.ShapeDtypeStruct((B, S, D), F32),
        compiler_params=_params(("parallel", "parallel")),
        name="pool_layer",
    )(x, x, x, g, w, b, sc)


def _gelu(x):
    return 0.5 * x * (1.0 + lax.erf(x * np.float32(np.sqrt(0.5))))


def _ffn_kernel(xm_ref, xp_ref, xn_ref, nw_ref, wg_ref, wv_ref, cw_ref, cb_ref, wd_ref, fn_ref,
                o_ref, u_ref, *, tm, final_norm):
    i = pl.program_id(1)
    j = pl.program_id(2)
    n = tm + 2 * HALO

    @pl.when(j == 0)
    def _():
        xm = xm_ref[0]
        g = nw_ref[0]
        u_ref[0:tm] = _rms(xm, g).astype(BF16)
        un = jnp.where(i < pl.num_programs(1) - 1, _rms(xn_ref[0], g), 0.0)
        up = jnp.where(i > 0, _rms(xp_ref[0], g), 0.0)
        u_ref[tm:n] = jnp.concatenate([un, up], axis=0).astype(BF16)
        o_ref[0] = xm

    gate = _dot(u_ref[...], wg_ref[0])
    val = _dot(u_ref[0:tm], wv_ref[0])
    cw = cw_ref[0]
    conv = (pltpu.roll(gate, 1, 0)[:tm] * cw[0:1] + gate[:tm] * cw[1:2]
            + pltpu.roll(gate, n - 1, 0)[:tm] * cw[2:3] + cb_ref[0])
    h = _gelu(conv) * val
    o_ref[0] += _dot(h.astype(BF16), wd_ref[0])

    if final_norm:
        @pl.when(j == pl.num_programs(2) - 1)
        def _():
            o_ref[0] = _rms(o_ref[0], fn_ref[...])


def _ffn_layer(x, layer, nw, w_up, cw, cb, w_down, fn, *, final_norm):
    B, S, D = x.shape
    tm = min(FFN_TOKENS, S)
    tf = FFN_CHUNK
    nj = D_FF // tf
    main, prev, nxt = _halo_specs(tm, S)
    return pl.pallas_call(
        functools.partial(_ffn_kernel, tm=tm, final_norm=final_norm),
        grid=(B, S // tm, nj),
        in_specs=[main, prev, nxt,
                  pl.BlockSpec((1, 1, D), lambda b_, i, j: (layer, 0, 0)),
                  pl.BlockSpec((1, D, tf), lambda b_, i, j: (layer, 0, j)),
                  pl.BlockSpec((1, D, tf), lambda b_, i, j: (layer, 0, nj + j)),
                  pl.BlockSpec((1, 3, tf), lambda b_, i, j: (layer, 0, j)),
                  pl.BlockSpec((1, 1, tf), lambda b_, i, j: (layer, 0, j)),
                  pl.BlockSpec((1, tf, D), lambda b_, i, j: (layer, j, 0)),
                  pl.BlockSpec((1, D), lambda b_, i, j: (0, 0))],
        out_specs=pl.BlockSpec((1, tm, D), lambda b_, i, j: (b_, i, 0)),
        out_shape=jax.ShapeDtypeStruct((B, S, D), F32),
        scratch_shapes=[pltpu.VMEM((tm + 2 * HALO, D), BF16)],
        compiler_params=_params(("parallel", "parallel", "arbitrary")),
        name="ffn_final" if final_norm else "ffn",
    )(x, x, x, nw, w_up, w_up, cw, cb, w_down, fn)


def _qkv_kernel(x_ref, nw_ref, w_ref, b_ref, s_ref, o_ref, u_ref):
    @pl.when(pl.program_id(2) == 0)
    def _():
        u_ref[...] = _rms(x_ref[0], nw_ref[...]).astype(BF16)

    res = ((_dot(u_ref[...], w_ref[...]) + b_ref[...]) * s_ref[...]).astype(BF16)
    for hh in range(o_ref.shape[1]):
        o_ref[0, hh] = res[:, hh * HEAD_DIM:(hh + 1) * HEAD_DIM]


def _qkv_layer(x, nw, w, b, s):
    B, S, D = x.shape
    tm = min(QKV_TOKENS, S)
    tn = QKV_CHUNK
    N = w.shape[1]
    return pl.pallas_call(
        _qkv_kernel,
        grid=(B, S // tm, N // tn),
        in_specs=[pl.BlockSpec((1, tm, D), lambda b_, i, j: (b_, i, 0)),
                  pl.BlockSpec((1, D), lambda b_, i, j: (0, 0)),
                  pl.BlockSpec((D, tn), lambda b_, i, j: (0, j)),
                  pl.BlockSpec((1, tn), lambda b_, i, j: (0, j)),
                  pl.BlockSpec((1, tn), lambda b_, i, j: (0, j))],
        out_specs=pl.BlockSpec((1, tn // HEAD_DIM, tm, HEAD_DIM), lambda b_, i, j: (b_, j, i, 0)),
        out_shape=jax.ShapeDtypeStruct((B, N // HEAD_DIM, S, HEAD_DIM), BF16),
        scratch_shapes=[pltpu.VMEM((tm, D), BF16)],
        compiler_params=_params(("parallel", "parallel", "arbitrary")),
        name="qkv",
    )(x, nw, w, b, s)


def _row_patterns():
    rows = 4 * ATT_ROWS
    kstart, tid = np.zeros((3, ATT_ROWS), int), np.zeros((3, ATT_ROWS), int)
    for v, r0 in enumerate((0, ATT_ROWS, rows - ATT_ROWS)):
        kb = np.clip(r0 - WIN_H // 2, 0, rows - ATT_KEY_ROWS)
        r = r0 + np.arange(ATT_ROWS)
        rs = np.clip(r - WIN_H // 2, 0, rows - WIN_H)
        kstart[v] = rs - kb
        tid[v] = rs - r + WIN_H - 1
    assert kstart.min() >= 0 and kstart.max() + WIN_H <= ATT_KEY_ROWS
    assert tid.min() >= 0 and tid.max() < WIN_H
    return kstart, tid


_ROW_KSTART, _ROW_TID = _row_patterns()


def _attn_bias_tables(rpb):
    H = rpb.shape[0]
    c = np.arange(GRID_W)
    cstart = np.clip(c - WIN_W // 2, 0, GRID_W - WIN_W)
    col_ok = (c[None, :] >= cstart[:, None]) & (c[None, :] < cstart[:, None] + WIN_W)
    span = 2 * GRID_W
    vi = np.clip(np.arange(span) - (GRID_W - 1), -(WIN_W - 1), WIN_W - 1) + WIN_W - 1
    vp = rpb[:, :, vi]
    skew = jnp.broadcast_to(vp[:, :, None, :], (H, 2 * WIN_H - 1, GRID_W, span))
    skew = skew.reshape(H, 2 * WIN_H - 1, GRID_W * span)[:, :, :GRID_W * (span - 1)]
    t = skew.reshape(H, 2 * WIN_H - 1, GRID_W, span - 1)[..., GRID_W - 1:]
    t = jnp.where(col_ok, t * LOG2_E, NEG_INF)
    t = t.transpose(0, 2, 1, 3).reshape(H, GRID_W, (2 * WIN_H - 1) * GRID_W)
    return jnp.stack([t[:, :, s * GRID_W:(s + WIN_H) * GRID_W] for s in range(WIN_H)], axis=1)


def _attn_kernel(q_ref, k_ref, v_ref, bias_ref, o_ref):
    i = pl.program_id(2)
    first = i == 0
    last = i == pl.num_programs(2) - 1
    nq = GRID_W
    nk = WIN_H * GRID_W

    def pick(tab, r):
        return jnp.where(first, int(tab[0, r]), jnp.where(last, int(tab[2, r]), int(tab[1, r])))

    def scores(hh, r):
        off = pl.multiple_of(pick(_ROW_KSTART, r) * GRID_W, GRID_W)
        q = q_ref[0, hh, r * nq:(r + 1) * nq, :]
        s = lax.dot_general(q, k_ref[0, hh, pl.ds(off, nk), :], (((1,), (1,)), ((), ())),
                            preferred_element_type=F32)
        return s + bias_ref[hh, pick(_ROW_TID, r)]

    def softmax(s):
        p = jnp.exp2(s - jnp.max(s, axis=-1, keepdims=True))
        return p.astype(BF16), jnp.sum(p, axis=-1, keepdims=True)

    def weighted(hh, r, p, l):
        off = pl.multiple_of(pick(_ROW_KSTART, r) * GRID_W, GRID_W)
        o = _dot(p, v_ref[0, hh, pl.ds(off, nk), :]) / l
        o_ref[0, hh, r * nq:(r + 1) * nq, :] = o.astype(BF16)

    work = [(hh, r) for hh in range(q_ref.shape[1]) for r in range(ATT_ROWS)]
    s_live, p_live = {}, {}
    d1, d2 = ATT_SKEW, 2 * ATT_SKEW
    for t in range(len(work) + d2):
        if t < len(work):
            s_live[t] = scores(*work[t])
        if 0 <= t - d1 < len(work):
            p_live[t - d1] = softmax(s_live.pop(t - d1))
        if 0 <= t - d2 < len(work):
            weighted(*work[t - d2], *p_live.pop(t - d2))


def _attn_layer(qkv, bias):
    B, _, S, _ = qkv.shape
    H, hb = N_HEADS, ATT_HEADS
    tq = ATT_ROWS * GRID_W
    tk = ATT_KEY_ROWS * GRID_W
    halo = (tk - tq) // 2
    nblk = S // tq
    assert nblk >= 3 and S % tq == 0 and tq % halo == 0
    ng = H // hb

    def keys(part):
        return pl.BlockSpec(
            (pl.Element(1), pl.Element(hb), pl.Element(tk), pl.Element(HEAD_DIM)),
            lambda b_, g, i: (b_, (part * ng + g) * hb,
                              halo * jnp.clip((tq // halo) * i - 1, 0, (S - tk) // halo), 0))

    nt = bias.shape[1]
    return pl.pallas_call(
        _attn_kernel,
        grid=(B, ng, nblk),
        in_specs=[pl.BlockSpec((1, hb, tq, HEAD_DIM), lambda b_, g, i: (b_, g, i, 0)),
                  keys(1), keys(2),
                  pl.BlockSpec((hb, nt, GRID_W, WIN_H * GRID_W), lambda b_, g, i: (g, 0, 0, 0))],
        out_specs=pl.BlockSpec((1, hb, tq, HEAD_DIM), lambda b_, g, i: (b_, g, i, 0)),
        out_shape=jax.ShapeDtypeStruct((B, H, S, HEAD_DIM), BF16),
        compiler_params=_params(("parallel", "parallel", "arbitrary")),
        name="natten",
    )(qkv, qkv, qkv, bias)


def _wo_kernel(o_ref, x_ref, w_ref, y_ref):
    o = jnp.concatenate([o_ref[0, h] for h in range(o_ref.shape[1])], axis=-1)
    y_ref[0] = x_ref[0] + _dot(o, w_ref[...])


def _wo_layer(o, x, w):
    B, S, D = x.shape
    tm = min(WO_TOKENS, S)
    tile = pl.BlockSpec((1, tm, D), lambda b_, i: (b_, i, 0))
    return pl.pallas_call(
        _wo_kernel,
        grid=(B, S // tm),
        in_specs=[pl.BlockSpec((1, N_HEADS, tm, HEAD_DIM), lambda b_, i: (b_, 0, i, 0)),
                  tile, pl.BlockSpec((D, D), lambda b_, i: (0, 0))],
        out_specs=tile,
        out_shape=jax.ShapeDtypeStruct((B, S, D), F32),
        compiler_params=_params(("parallel", "parallel")),
        name="attn_out",
    )(o, x, w)


def _prep_weights(mix_norm, pool_w, pool_b, pool_scale, attn_w_qkv, attn_b_qkv, attn_rpb, attn_w_o,
                  ffn_norm, ffn_w_up, ffn_conv_w, ffn_conv_b, ffn_w_down, final_norm):
    assert pool_w.shape[0] == 1 and attn_w_qkv.shape[0] == 1 and ffn_w_up.shape[0] == 2
    q_scale = jnp.concatenate([jnp.full((D_MODEL,), HEAD_DIM ** -0.5 * LOG2_E, F32),
                               jnp.ones((2 * D_MODEL,), F32)]).reshape(1, 3 * D_MODEL)
    return dict(
        mix_norm=mix_norm.reshape(-1, 1, D_MODEL),
        pool_w=pool_w[0].astype(BF16),
        pool_b=pool_b[0].reshape(N_POOL_GROUPS, 1, GROUP_DIM),
        pool_scale=pool_scale[0].reshape(1, D_MODEL),
        w_qkv=attn_w_qkv[0].astype(BF16),
        b_qkv=attn_b_qkv[0].reshape(1, 3 * D_MODEL),
        q_scale=q_scale,
        bias=_attn_bias_tables(attn_rpb[0]),
        w_o=attn_w_o[0].astype(BF16),
        ffn=(ffn_norm.reshape(-1, 1, D_MODEL), ffn_w_up.astype(BF16), ffn_conv_w,
             ffn_conv_b.reshape(-1, 1, D_FF), ffn_w_down.astype(BF16)),
        final_norm=final_norm.reshape(1, D_MODEL),
    )


def _trunk(x, p):
    x = _pool_layer(x, p["mix_norm"][0], p["pool_w"], p["pool_b"], p["pool_scale"])
    x = _ffn_layer(x, 0, *p["ffn"], p["final_norm"], final_norm=False)
    qkv = _qkv_layer(x, p["mix_norm"][1], p["w_qkv"], p["b_qkv"], p["q_scale"])
    o = _attn_layer(qkv, p["bias"])
    x = _wo_layer(o, x, p["w_o"])
    return _ffn_layer(x, 1, *p["ffn"], p["final_norm"], final_norm=True)


def kernel(x_prompt, x_sample, mix_norm, pool_w, pool_b, pool_scale, attn_w_qkv, attn_b_qkv, attn_rpb,
           attn_w_o, ffn_norm, ffn_w_up, ffn_conv_w, ffn_conv_b, ffn_w_down, final_norm):
    p = _prep_weights(mix_norm, pool_w, pool_b, pool_scale, attn_w_qkv, attn_b_qkv, attn_rpb, attn_w_o,
                      ffn_norm, ffn_w_up, ffn_conv_w, ffn_conv_b, ffn_w_down, final_norm)
    return _trunk(x_prompt, p), _trunk(x_sample, p)
```

```python
import functools

import numpy as np
import jax
import jax.numpy as jnp
from jax import lax
from jax.experimental import pallas as pl
from jax.experimental.pallas import tpu as pltpu

D_MODEL = 2048
GRID_W = 64
N_POOL_GROUPS = 4
POOL_WINDOWS = (2, 4, 8, 16)
GROUP_DIM = D_MODEL // N_POOL_GROUPS
N_HEADS = 16
HEAD_DIM = D_MODEL // N_HEADS
WIN_H = 8
WIN_W = 16
D_FF = 5632
EPS = 1e-6
NEG_INF = -1e30
LOG2_E = float(np.log2(np.e))

HALO = 8
POOL_TOKENS = 512
FFN_TOKENS = 1024
FFN_CHUNK = 512
QKV_TOKENS = 1024
QKV_CHUNK = 2048
WO_TOKENS = 1024
ATT_ROWS = 32
ATT_HEADS = 2
ATT_KEY_ROWS = ATT_ROWS + WIN_H
ATT_SKEW = 4
VMEM_LIMIT = 60 * 1024 * 1024

F32 = jnp.float32
BF16 = jnp.bfloat16


def _rms(x, g):
    return x * lax.rsqrt(jnp.mean(x * x, axis=-1, keepdims=True) + EPS) * g


def _dot(a, b):
    return jnp.dot(a, b, preferred_element_type=F32)


def _params(sem):
    return pltpu.CompilerParams(dimension_semantics=sem, vmem_limit_bytes=VMEM_LIMIT)


def _halo_specs(tokens, seq):
    per = tokens // HALO
    last = seq // HALO - 1
    main = pl.BlockSpec((1, tokens, D_MODEL), lambda b, i, *_: (b, i, 0))
    prev = pl.BlockSpec((1, HALO, D_MODEL), lambda b, i, *_: (b, jnp.maximum(i * per - 1, 0), 0))
    nxt = pl.BlockSpec((1, HALO, D_MODEL), lambda b, i, *_: (b, jnp.minimum((i + 1) * per, last), 0))
    return main, prev, nxt


def _pool_kernel(xm_ref, xp_ref, xn_ref, g_ref, w_ref, b_ref, sc_ref, o_ref, *, seq, tp):
    i = pl.program_id(1)
    n = tp + 2 * HALO
    xm = xm_ref[0]
    g_all = g_ref[...]
    up = jnp.where(i > 0, _rms(xp_ref[0], g_all), 0.0)
    un = jnp.where(i < pl.num_programs(1) - 1, _rms(xn_ref[0], g_all), 0.0)
    u = jnp.concatenate([up, _rms(xm, g_all), un], axis=0)
    posm = i * tp + lax.broadcasted_iota(jnp.int32, (tp, 1), 0)
    for g, w in enumerate(POOL_WINDOWS):
        cols = slice(g * GROUP_DIM, (g + 1) * GROUP_DIM)
        ug = u[:, cols]
        a = ug + pltpu.roll(ug, 1, 0)
        width = 2
        while width < w:
            a = a + pltpu.roll(a, width, 0)
            width *= 2
        s = a if w == 2 else pltpu.roll(a, n - (w // 2 - 1), 0)
        lo = jnp.clip(posm - w // 2, 0, seq)
        hi = jnp.clip(posm - w // 2 + w, 0, seq)
        inv_cnt = 1.0 / (hi - lo).astype(F32)
        d = s[HALO:HALO + tp] * inv_cnt - ug[HALO:HALO + tp]
        y = _dot(d.astype(BF16), w_ref[g]) + b_ref[g]
        o_ref[0, :, cols] = xm[:, cols] + y * sc_ref[:, cols]


def _pool_layer(x, g, w, b, sc):
    B, S, D = x.shape
    tp = min(POOL_TOKENS, S)
    main, prev, nxt = _halo_specs(tp, S)
    const2 = lambda b_, i: (0, 0)
    const3 = lambda b_, i: (0, 0, 0)
    return pl.pallas_call(
        functools.partial(_pool_kernel, seq=S, tp=tp),
        grid=(B, S // tp),
        in_specs=[main, prev, nxt,
                  pl.BlockSpec((1, D), const2),
                  pl.BlockSpec((N_POOL_GROUPS, GROUP_DIM, GROUP_DIM), const3),
                  pl.BlockSpec((N_POOL_GROUPS, 1, GROUP_DIM), const3),
                  pl.BlockSpec((1, D), const2)],
        out_specs=pl.BlockSpec((1, tp, D), lambda b_, i: (b_, i, 0)),
        out_shape=jax.ShapeDtypeStruct((B, S, D), F32),
        compiler_params=_params(("parallel", "parallel")),
        name="pool_layer",
    )(x, x, x, g, w, b, sc)


def _gelu(x):
    return 0.5 * x * (1.0 + lax.erf(x * np.float32(np.sqrt(0.5))))


def _ffn_kernel(xm_ref, xp_ref, xn_ref, nw_ref, wg_ref, wv_ref, cw_ref, cb_ref, wd_ref, fn_ref,
                o_ref, u_ref, *, tm, final_norm):
    i = pl.program_id(1)
    j = pl.program_id(2)
    n = tm + 2 * HALO

    @pl.when(j == 0)
    def _():
        xm = xm_ref[0]
        g = nw_ref[0]
        u_ref[0:tm] = _rms(xm, g).astype(BF16)
        un = jnp.where(i < pl.num_programs(1) - 1, _rms(xn_ref[0], g), 0.0)
        up = jnp.where(i > 0, _rms(xp_ref[0], g), 0.0)
        u_ref[tm:n] = jnp.concatenate([un, up], axis=0).astype(BF16)
        o_ref[0] = xm

    gate = _dot(u_ref[...], wg_ref[0])
    val = _dot(u_ref[0:tm], wv_ref[0])
    cw = cw_ref[0]
    conv = (pltpu.roll(gate, 1, 0)[:tm] * cw[0:1] + gate[:tm] * cw[1:2]
            + pltpu.roll(gate, n - 1, 0)[:tm] * cw[2:3] + cb_ref[0])
    h = _gelu(conv) * val
    o_ref[0] += _dot(h.astype(BF16), wd_ref[0])

    if final_norm:
        @pl.when(j == pl.num_programs(2) - 1)
        def _():
            o_ref[0] = _rms(o_ref[0], fn_ref[...])


def _ffn_layer(x, layer, nw, w_up, cw, cb, w_down, fn, *, final_norm):
    B, S, D = x.shape
    tm = min(FFN_TOKENS, S)
    tf = FFN_CHUNK
    nj = D_FF // tf
    main, prev, nxt = _halo_specs(tm, S)
    return pl.pallas_call(
        functools.partial(_ffn_kernel, tm=tm, final_norm=final_norm),
        grid=(B, S // tm, nj),
        in_specs=[main, prev, nxt,
                  pl.BlockSpec((1, 1, D), lambda b_, i, j: (layer, 0, 0)),
                  pl.BlockSpec((1, D, tf), lambda b_, i, j: (layer, 0, j)),
                  pl.BlockSpec((1, D, tf), lambda b_, i, j: (layer, 0, nj + j)),
                  pl.BlockSpec((1, 3, tf), lambda b_, i, j: (layer, 0, j)),
                  pl.BlockSpec((1, 1, tf), lambda b_, i, j: (layer, 0, j)),
                  pl.BlockSpec((1, tf, D), lambda b_, i, j: (layer, j, 0)),
                  pl.BlockSpec((1, D), lambda b_, i, j: (0, 0))],
        out_specs=pl.BlockSpec((1, tm, D), lambda b_, i, j: (b_, i, 0)),
        out_shape=jax.ShapeDtypeStruct((B, S, D), F32),
        scratch_shapes=[pltpu.VMEM((tm + 2 * HALO, D), BF16)],
        compiler_params=_params(("parallel", "parallel", "arbitrary")),
        name="ffn_final" if final_norm else "ffn",
    )(x, x, x, nw, w_up, w_up, cw, cb, w_down, fn)


def _qkv_kernel(x_ref, nw_ref, w_ref, b_ref, s_ref, o_ref, u_ref):
    @pl.when(pl.program_id(2) == 0)
    def _():
        u_ref[...] = _rms(x_ref[0], nw_ref[...]).astype(BF16)

    res = ((_dot(u_ref[...], w_ref[...]) + b_ref[...]) * s_ref[...]).astype(BF16)
    for hh in range(o_ref.shape[1]):
        o_ref[0, hh] = res[:, hh * HEAD_DIM:(hh + 1) * HEAD_DIM]


def _qkv_layer(x, nw, w, b, s):
    B, S, D = x.shape
    tm = min(QKV_TOKENS, S)
    tn = QKV_CHUNK
    N = w.shape[1]
    return pl.pallas_call(
        _qkv_kernel,
        grid=(B, S // tm, N // tn),
        in_specs=[pl.BlockSpec((1, tm, D), lambda b_, i, j: (b_, i, 0)),
                  pl.BlockSpec((1, D), lambda b_, i, j: (0, 0)),
                  pl.BlockSpec((D, tn), lambda b_, i, j: (0, j)),
                  pl.BlockSpec((1, tn), lambda b_, i, j: (0, j)),
                  pl.BlockSpec((1, tn), lambda b_, i, j: (0, j))],
        out_specs=pl.BlockSpec((1, tn // HEAD_DIM, tm, HEAD_DIM), lambda b_, i, j: (b_, j, i, 0)),
        out_shape=jax.ShapeDtypeStruct((B, N // HEAD_DIM, S, HEAD_DIM), BF16),
        scratch_shapes=[pltpu.VMEM((tm, D), BF16)],
        compiler_params=_params(("parallel", "parallel", "arbitrary")),
        name="qkv",
    )(x, nw, w, b, s)


def _row_patterns():
    rows = 4 * ATT_ROWS
    kstart, tid = np.zeros((3, ATT_ROWS), int), np.zeros((3, ATT_ROWS), int)
    for v, r0 in enumerate((0, ATT_ROWS, rows - ATT_ROWS)):
        kb = np.clip(r0 - WIN_H // 2, 0, rows - ATT_KEY_ROWS)
        r = r0 + np.arange(ATT_ROWS)
        rs = np.clip(r - WIN_H // 2, 0, rows - WIN_H)
        kstart[v] = rs - kb
        tid[v] = rs - r + WIN_H - 1
    assert kstart.min() >= 0 and kstart.max() + WIN_H <= ATT_KEY_ROWS
    assert tid.min() >= 0 and tid.max() < WIN_H
    return kstart, tid


_ROW_KSTART, _ROW_TID = _row_patterns()


def _attn_bias_tables(rpb):
    H = rpb.shape[0]
    c = np.arange(GRID_W)
    cstart = np.clip(c - WIN_W // 2, 0, GRID_W - WIN_W)
    col_ok = (c[None, :] >= cstart[:, None]) & (c[None, :] < cstart[:, None] + WIN_W)
    span = 2 * GRID_W
    vi = np.clip(np.arange(span) - (GRID_W - 1), -(WIN_W - 1), WIN_W - 1) + WIN_W - 1
    vp = rpb[:, :, vi]
    skew = jnp.broadcast_to(vp[:, :, None, :], (H, 2 * WIN_H - 1, GRID_W, span))
    skew = skew.reshape(H, 2 * WIN_H - 1, GRID_W * span)[:, :, :GRID_W * (span - 1)]
    t = skew.reshape(H, 2 * WIN_H - 1, GRID_W, span - 1)[..., GRID_W - 1:]
    t = jnp.where(col_ok, t * LOG2_E, NEG_INF)
    t = t.transpose(0, 2, 1, 3).reshape(H, GRID_W, (2 * WIN_H - 1) * GRID_W)
    return jnp.stack([t[:, :, s * GRID_W:(s + WIN_H) * GRID_W] for s in range(WIN_H)], axis=1)


def _attn_kernel(q_ref, k_ref, v_ref, bias_ref, o_ref):
    i = pl.program_id(2)
    first = i == 0
    last = i == pl.num_programs(2) - 1
    nq = GRID_W
    nk = WIN_H * GRID_W

    def pick(tab, r):
        return jnp.where(first, int(tab[0, r]), jnp.where(last, int(tab[2, r]), int(tab[1, r])))

    def scores(hh, r):
        off = pl.multiple_of(pick(_ROW_KSTART, r) * GRID_W, GRID_W)
        q = q_ref[0, hh, r * nq:(r + 1) * nq, :]
        s = lax.dot_general(q, k_ref[0, hh, pl.ds(off, nk), :], (((1,), (1,)), ((), ())),
                            preferred_element_type=F32)
        return s + bias_ref[hh, pick(_ROW_TID, r)]

    def softmax(s):
        p = jnp.exp2(s - jnp.max(s, axis=-1, keepdims=True))
        return p.astype(BF16), jnp.sum(p, axis=-1, keepdims=True)

    def weighted(hh, r, p, l):
        off = pl.multiple_of(pick(_ROW_KSTART, r) * GRID_W, GRID_W)
        o = _dot(p, v_ref[0, hh, pl.ds(off, nk), :]) / l
        o_ref[0, hh, r * nq:(r + 1) * nq, :] = o.astype(BF16)

    work = [(hh, r) for hh in range(q_ref.shape[1]) for r in range(ATT_ROWS)]
    s_live, p_live = {}, {}
    d1, d2 = ATT_SKEW, 2 * ATT_SKEW
    for t in range(len(work) + d2):
        if t < len(work):
            s_live[t] = scores(*work[t])
        if 0 <= t - d1 < len(work):
            p_live[t - d1] = softmax(s_live.pop(t - d1))
        if 0 <= t - d2 < len(work):
            weighted(*work[t - d2], *p_live.pop(t - d2))


def _attn_layer(qkv, bias):
    B, _, S, _ = qkv.shape
    H, hb = N_HEADS, ATT_HEADS
    tq = ATT_ROWS * GRID_W
    tk = ATT_KEY_ROWS * GRID_W
    halo = (tk - tq) // 2
    nblk = S // tq
    assert nblk >= 3 and S % tq == 0 and tq % halo == 0
    ng = H // hb

    def keys(part):
        return pl.BlockSpec(
            (pl.Element(1), pl.Element(hb), pl.Element(tk), pl.Element(HEAD_DIM)),
            lambda b_, g, i: (b_, (part * ng + g) * hb,
                              halo * jnp.clip((tq // halo) * i - 1, 0, (S - tk) // halo), 0))

    nt = bias.shape[1]
    return pl.pallas_call(
        _attn_kernel,
        grid=(B, ng, nblk),
        in_specs=[pl.BlockSpec((1, hb, tq, HEAD_DIM), lambda b_, g, i: (b_, g, i, 0)),
                  keys(1), keys(2),
                  pl.BlockSpec((hb, nt, GRID_W, WIN_H * GRID_W), lambda b_, g, i: (g, 0, 0, 0))],
        out_specs=pl.BlockSpec((1, hb, tq, HEAD_DIM), lambda b_, g, i: (b_, g, i, 0)),
        out_shape=jax.ShapeDtypeStruct((B, H, S, HEAD_DIM), BF16),
        compiler_params=_params(("parallel", "parallel", "arbitrary")),
        name="natten",
    )(qkv, qkv, qkv, bias)


def _wo_kernel(o_ref, x_ref, w_ref, y_ref):
    o = jnp.concatenate([o_ref[0, h] for h in range(o_ref.shape[1])], axis=-1)
    y_ref[0] = x_ref[0] + _dot(o, w_ref[...])


def _wo_layer(o, x, w):
    B, S, D = x.shape
    tm = min(WO_TOKENS, S)
    tile = pl.BlockSpec((1, tm, D), lambda b_, i: (b_, i, 0))
    return pl.pallas_call(
        _wo_kernel,
        grid=(B, S // tm),
        in_specs=[pl.BlockSpec((1, N_HEADS, tm, HEAD_DIM), lambda b_, i: (b_, 0, i, 0)),
                  tile,
                  pl.BlockSpec((D, D), lambda b_, i: (0, 0), pipeline_mode=pl.Buffered(1))],
        out_specs=tile,
        out_shape=jax.ShapeDtypeStruct((B, S, D), F32),
        compiler_params=_params(("parallel", "parallel")),
        name="attn_out",
    )(o, x, w)


def _prep_weights(mix_norm, pool_w, pool_b, pool_scale, attn_w_qkv, attn_b_qkv, attn_rpb, attn_w_o,
                  ffn_norm, ffn_w_up, ffn_conv_w, ffn_conv_b, ffn_w_down, final_norm):
    assert pool_w.shape[0] == 1 and attn_w_qkv.shape[0] == 1 and ffn_w_up.shape[0] == 2
    q_scale = jnp.concatenate([jnp.full((D_MODEL,), HEAD_DIM ** -0.5 * LOG2_E, F32),
                               jnp.ones((2 * D_MODEL,), F32)]).reshape(1, 3 * D_MODEL)
    return dict(
        mix_norm=mix_norm.reshape(-1, 1, D_MODEL),
        pool_w=pool_w[0].astype(BF16),
        pool_b=pool_b[0].reshape(N_POOL_GROUPS, 1, GROUP_DIM),
        pool_scale=pool_scale[0].reshape(1, D_MODEL),
        w_qkv=attn_w_qkv[0].astype(BF16),
        b_qkv=attn_b_qkv[0].reshape(1, 3 * D_MODEL),
        q_scale=q_scale,
        bias=_attn_bias_tables(attn_rpb[0]),
        w_o=attn_w_o[0].astype(BF16),
        ffn=(ffn_norm.reshape(-1, 1, D_MODEL), ffn_w_up.astype(BF16), ffn_conv_w,
             ffn_conv_b.reshape(-1, 1, D_FF), ffn_w_down.astype(BF16)),
        final_norm=final_norm.reshape(1, D_MODEL),
    )


def _trunk(x, p):
    x = _pool_layer(x, p["mix_norm"][0], p["pool_w"], p["pool_b"], p["pool_scale"])
    x = _ffn_layer(x, 0, *p["ffn"], p["final_norm"], final_norm=False)
    qkv = _qkv_layer(x, p["mix_norm"][1], p["w_qkv"], p["b_qkv"], p["q_scale"])
    o = _attn_layer(qkv, p["bias"])
    x = _wo_layer(o, x, p["w_o"])
    return _ffn_layer(x, 1, *p["ffn"], p["final_norm"], final_norm=True)


def kernel(x_prompt, x_sample, mix_norm, pool_w, pool_b, pool_scale, attn_w_qkv, attn_b_qkv, attn_rpb,
           attn_w_o, ffn_norm, ffn_w_up, ffn_conv_w, ffn_conv_b, ffn_w_down, final_norm):
    p = _prep_weights(mix_norm, pool_w, pool_b, pool_scale, attn_w_qkv, attn_b_qkv, attn_rpb, attn_w_o,
                      ffn_norm, ffn_w_up, ffn_conv_w, ffn_conv_b, ffn_w_down, final_norm)
    return _trunk(x_prompt, p), _trunk(x_sample, p)
```

```python
import functools

import numpy as np
import jax
import jax.numpy as jnp
from jax import lax
from jax.experimental import pallas as pl
from jax.experimental.pallas import tpu as pltpu

D_MODEL = 2048
GRID_W = 64
N_POOL_GROUPS = 4
POOL_WINDOWS = (2, 4, 8, 16)
GROUP_DIM = D_MODEL // N_POOL_GROUPS
N_HEADS = 16
HEAD_DIM = D_MODEL // N_HEADS
WIN_H = 8
WIN_W = 16
D_FF = 5632
EPS = 1e-6
NEG_INF = -1e30
LOG2_E = float(np.log2(np.e))

HALO = 8
POOL_TOKENS = 512
FFN_TOKENS = 1024
FFN_CHUNK = 512
QKV_TOKENS = 1024
QKV_CHUNK = 2048
WO_TOKENS = 1024
ATT_ROWS = 32
ATT_HEADS = 2
ATT_KEY_ROWS = ATT_ROWS + WIN_H
ATT_SKEW_S = 1
ATT_SKEW = 2
VMEM_LIMIT = 60 * 1024 * 1024

F32 = jnp.float32
BF16 = jnp.bfloat16


def _rms(x, g):
    return x * lax.rsqrt(jnp.mean(x * x, axis=-1, keepdims=True) + EPS) * g


def _dot(a, b):
    return jnp.dot(a, b, preferred_element_type=F32)


def _params(sem):
    return pltpu.CompilerParams(dimension_semantics=sem, vmem_limit_bytes=VMEM_LIMIT)


def _halo_specs(tokens, seq):
    per = tokens // HALO
    last = seq // HALO - 1
    main = pl.BlockSpec((1, tokens, D_MODEL), lambda b, i, *_: (b, i, 0))
    prev = pl.BlockSpec((1, HALO, D_MODEL), lambda b, i, *_: (b, jnp.maximum(i * per - 1, 0), 0))
    nxt = pl.BlockSpec((1, HALO, D_MODEL), lambda b, i, *_: (b, jnp.minimum((i + 1) * per, last), 0))
    return main, prev, nxt


def _pool_kernel(xm_ref, xp_ref, xn_ref, g_ref, w_ref, b_ref, sc_ref, o_ref, *, seq, tp):
    i = pl.program_id(1)
    n = tp + 2 * HALO
    xm = xm_ref[0]
    g_all = g_ref[...]
    up = jnp.where(i > 0, _rms(xp_ref[0], g_all), 0.0)
    un = jnp.where(i < pl.num_programs(1) - 1, _rms(xn_ref[0], g_all), 0.0)
    u = jnp.concatenate([up, _rms(xm, g_all), un], axis=0)
    posm = i * tp + lax.broadcasted_iota(jnp.int32, (tp, 1), 0)
    for g, w in enumerate(POOL_WINDOWS):
        cols = slice(g * GROUP_DIM, (g + 1) * GROUP_DIM)
        ug = u[:, cols]
        a = ug + pltpu.roll(ug, 1, 0)
        width = 2
        while width < w:
            a = a + pltpu.roll(a, width, 0)
            width *= 2
        s = a if w == 2 else pltpu.roll(a, n - (w // 2 - 1), 0)
        lo = jnp.clip(posm - w // 2, 0, seq)
        hi = jnp.clip(posm - w // 2 + w, 0, seq)
        inv_cnt = 1.0 / (hi - lo).astype(F32)
        d = s[HALO:HALO + tp] * inv_cnt - ug[HALO:HALO + tp]
        y = _dot(d.astype(BF16), w_ref[g]) + b_ref[g]
        o_ref[0, :, cols] = xm[:, cols] + y * sc_ref[:, cols]


def _pool_layer(x, g, w, b, sc):
    B, S, D = x.shape
    tp = min(POOL_TOKENS, S)
    main, prev, nxt = _halo_specs(tp, S)
    const2 = lambda b_, i: (0, 0)
    const3 = lambda b_, i: (0, 0, 0)
    return pl.pallas_call(
        functools.partial(_pool_kernel, seq=S, tp=tp),
        grid=(B, S // tp),
        in_specs=[main, prev, nxt,
                  pl.BlockSpec((1, D), const2),
                  pl.BlockSpec((N_POOL_GROUPS, GROUP_DIM, GROUP_DIM), const3),
                  pl.BlockSpec((N_POOL_GROUPS, 1, GROUP_DIM), const3),
                  pl.BlockSpec((1, D), const2)],
        out_specs=pl.BlockSpec((1, tp, D), lambda b_, i: (b_, i, 0)),
        out_shape=jax.ShapeDtypeStruct((B, S, D), F32),
        compiler_params=_params(("parallel", "parallel")),
        name="pool_layer",
    )(x, x, x, g, w, b, sc)


def _gelu(x):
    return 0.5 * x * (1.0 + lax.erf(x * np.float32(np.sqrt(0.5))))


def _ffn_kernel(xm_ref, xp_ref, xn_ref, nw_ref, wg_ref, wv_ref, cw_ref, cb_ref, wd_ref, fn_ref,
                o_ref, u_ref, *, tm, final_norm):
    i = pl.program_id(1)
    j = pl.program_id(2)
    n = tm + 2 * HALO

    @pl.when(j == 0)
    def _():
        xm = xm_ref[0]
        g = nw_ref[0]
        u_ref[0:tm] = _rms(xm, g).astype(BF16)
        un = jnp.where(i < pl.num_programs(1) - 1, _rms(xn_ref[0], g), 0.0)
        up = jnp.where(i > 0, _rms(xp_ref[0], g), 0.0)
        u_ref[tm:n] = jnp.concatenate([un, up], axis=0).astype(BF16)
        o_ref[0] = xm

    gate = _dot(u_ref[...], wg_ref[0])
    val = _dot(u_ref[0:tm], wv_ref[0])
    cw = cw_ref[0]
    conv = (pltpu.roll(gate, 1, 0)[:tm] * cw[0:1] + gate[:tm] * cw[1:2]
            + pltpu.roll(gate, n - 1, 0)[:tm] * cw[2:3] + cb_ref[0])
    h = _gelu(conv) * val
    o_ref[0] += _dot(h.astype(BF16), wd_ref[0])

    if final_norm:
        @pl.when(j == pl.num_programs(2) - 1)
        def _():
            o_ref[0] = _rms(o_ref[0], fn_ref[...])


def _ffn_layer(x, layer, nw, w_up, cw, cb, w_down, fn, *, final_norm):
    B, S, D = x.shape
    tm = min(FFN_TOKENS, S)
    tf = FFN_CHUNK
    nj = D_FF // tf
    main, prev, nxt = _halo_specs(tm, S)
    return pl.pallas_call(
        functools.partial(_ffn_kernel, tm=tm, final_norm=final_norm),
        grid=(B, S // tm, nj),
        in_specs=[main, prev, nxt,
                  pl.BlockSpec((1, 1, D), lambda b_, i, j: (layer, 0, 0)),
                  pl.BlockSpec((1, D, tf), lambda b_, i, j: (layer, 0, j)),
                  pl.BlockSpec((1, D, tf), lambda b_, i, j: (layer, 0, nj + j)),
                  pl.BlockSpec((1, 3, tf), lambda b_, i, j: (layer, 0, j)),
                  pl.BlockSpec((1, 1, tf), lambda b_, i, j: (layer, 0, j)),
                  pl.BlockSpec((1, tf, D), lambda b_, i, j: (layer, j, 0)),
                  pl.BlockSpec((1, D), lambda b_, i, j: (0, 0))],
        out_specs=pl.BlockSpec((1, tm, D), lambda b_, i, j: (b_, i, 0)),
        out_shape=jax.ShapeDtypeStruct((B, S, D), F32),
        scratch_shapes=[pltpu.VMEM((tm + 2 * HALO, D), BF16)],
        compiler_params=_params(("parallel", "parallel", "arbitrary")),
        name="ffn_final" if final_norm else "ffn",
    )(x, x, x, nw, w_up, w_up, cw, cb, w_down, fn)


def _qkv_kernel(x_ref, nw_ref, w_ref, b_ref, s_ref, o_ref, u_ref):
    @pl.when(pl.program_id(2) == 0)
    def _():
        u_ref[...] = _rms(x_ref[0], nw_ref[...]).astype(BF16)

    res = ((_dot(u_ref[...], w_ref[...]) + b_ref[...]) * s_ref[...]).astype(BF16)
    for hh in range(o_ref.shape[1]):
        o_ref[0, hh] = res[:, hh * HEAD_DIM:(hh + 1) * HEAD_DIM]


def _qkv_layer(x, nw, w, b, s):
    B, S, D = x.shape
    tm = min(QKV_TOKENS, S)
    tn = QKV_CHUNK
    N = w.shape[1]
    return pl.pallas_call(
        _qkv_kernel,
        grid=(B, S // tm, N // tn),
        in_specs=[pl.BlockSpec((1, tm, D), lambda b_, i, j: (b_, i, 0)),
                  pl.BlockSpec((1, D), lambda b_, i, j: (0, 0)),
                  pl.BlockSpec((D, tn), lambda b_, i, j: (0, j)),
                  pl.BlockSpec((1, tn), lambda b_, i, j: (0, j)),
                  pl.BlockSpec((1, tn), lambda b_, i, j: (0, j))],
        out_specs=pl.BlockSpec((1, tn // HEAD_DIM, tm, HEAD_DIM), lambda b_, i, j: (b_, j, i, 0)),
        out_shape=jax.ShapeDtypeStruct((B, N // HEAD_DIM, S, HEAD_DIM), BF16),
        scratch_shapes=[pltpu.VMEM((tm, D), BF16)],
        compiler_params=_params(("parallel", "parallel", "arbitrary")),
        name="qkv",
    )(x, nw, w, b, s)


def _row_patterns():
    rows = 4 * ATT_ROWS
    kstart, tid = np.zeros((3, ATT_ROWS), int), np.zeros((3, ATT_ROWS), int)
    for v, r0 in enumerate((0, ATT_ROWS, rows - ATT_ROWS)):
        kb = np.clip(r0 - WIN_H // 2, 0, rows - ATT_KEY_ROWS)
        r = r0 + np.arange(ATT_ROWS)
        rs = np.clip(r - WIN_H // 2, 0, rows - WIN_H)
        kstart[v] = rs - kb
        tid[v] = rs - r + WIN_H - 1
    assert kstart.min() >= 0 and kstart.max() + WIN_H <= ATT_KEY_ROWS
    assert tid.min() >= 0 and tid.max() < WIN_H
    return kstart, tid


_ROW_KSTART, _ROW_TID = _row_patterns()


def _attn_bias_tables(rpb):
    H = rpb.shape[0]
    c = np.arange(GRID_W)
    cstart = np.clip(c - WIN_W // 2, 0, GRID_W - WIN_W)
    col_ok = (c[None, :] >= cstart[:, None]) & (c[None, :] < cstart[:, None] + WIN_W)
    span = 2 * GRID_W
    vi = np.clip(np.arange(span) - (GRID_W - 1), -(WIN_W - 1), WIN_W - 1) + WIN_W - 1
    vp = rpb[:, :, vi]
    skew = jnp.broadcast_to(vp[:, :, None, :], (H, 2 * WIN_H - 1, GRID_W, span))
    skew = skew.reshape(H, 2 * WIN_H - 1, GRID_W * span)[:, :, :GRID_W * (span - 1)]
    t = skew.reshape(H, 2 * WIN_H - 1, GRID_W, span - 1)[..., GRID_W - 1:]
    t = jnp.where(col_ok, t * LOG2_E, NEG_INF)
    t = t.transpose(0, 2, 1, 3).reshape(H, GRID_W, (2 * WIN_H - 1) * GRID_W)
    return jnp.stack([t[:, :, s * GRID_W:(s + WIN_H) * GRID_W] for s in range(WIN_H)], axis=1)


def _attn_kernel(q_ref, k_ref, v_ref, bias_ref, o_ref):
    i = pl.program_id(2)
    first = i == 0
    last = i == pl.num_programs(2) - 1
    nq = GRID_W
    nk = WIN_H * GRID_W

    def pick(tab, r):
        return jnp.where(first, int(tab[0, r]), jnp.where(last, int(tab[2, r]), int(tab[1, r])))

    def scores(hh, r):
        off = pl.multiple_of(pick(_ROW_KSTART, r) * GRID_W, GRID_W)
        q = q_ref[0, hh, r * nq:(r + 1) * nq, :]
        s = lax.dot_general(q, k_ref[0, hh, pl.ds(off, nk), :], (((1,), (1,)), ((), ())),
                            preferred_element_type=F32)
        return s + bias_ref[hh, pick(_ROW_TID, r)]

    def softmax(s):
        p = jnp.exp2(s - jnp.max(s, axis=-1, keepdims=True))
        return p.astype(BF16), jnp.sum(p, axis=-1, keepdims=True)

    def weighted(hh, r, p, l):
        off = pl.multiple_of(pick(_ROW_KSTART, r) * GRID_W, GRID_W)
        o = _dot(p, v_ref[0, hh, pl.ds(off, nk), :]) / l
        o_ref[0, hh, r * nq:(r + 1) * nq, :] = o.astype(BF16)

    work = [(hh, r) for hh in range(q_ref.shape[1]) for r in range(ATT_ROWS)]
    s_live, p_live = {}, {}
    d1, d2 = ATT_SKEW_S, ATT_SKEW_S + ATT_SKEW
    for t in range(len(work) + d2):
        if t < len(work):
            s_live[t] = scores(*work[t])
        if 0 <= t - d1 < len(work):
            p_live[t - d1] = softmax(s_live.pop(t - d1))
        if 0 <= t - d2 < len(work):
            weighted(*work[t - d2], *p_live.pop(t - d2))


def _attn_layer(qkv, bias):
    B, _, S, _ = qkv.shape
    H, hb = N_HEADS, ATT_HEADS
    tq = ATT_ROWS * GRID_W
    tk = ATT_KEY_ROWS * GRID_W
    halo = (tk - tq) // 2
    nblk = S // tq
    assert nblk >= 3 and S % tq == 0 and tq % halo == 0
    ng = H // hb

    def keys(part):
        return pl.BlockSpec(
            (pl.Element(1), pl.Element(hb), pl.Element(tk), pl.Element(HEAD_DIM)),
            lambda b_, g, i: (b_, (part * ng + g) * hb,
                              halo * jnp.clip((tq // halo) * i - 1, 0, (S - tk) // halo), 0))

    nt = bias.shape[1]
    return pl.pallas_call(
        _attn_kernel,
        grid=(B, ng, nblk),
        in_specs=[pl.BlockSpec((1, hb, tq, HEAD_DIM), lambda b_, g, i: (b_, g, i, 0)),
                  keys(1), keys(2),
                  pl.BlockSpec((hb, nt, GRID_W, WIN_H * GRID_W), lambda b_, g, i: (g, 0, 0, 0))],
        out_specs=pl.BlockSpec((1, hb, tq, HEAD_DIM), lambda b_, g, i: (b_, g, i, 0)),
        out_shape=jax.ShapeDtypeStruct((B, H, S, HEAD_DIM), BF16),
        compiler_params=_params(("parallel", "parallel", "arbitrary")),
        name="natten",
    )(qkv, qkv, qkv, bias)


def _wo_kernel(o_ref, x_ref, w_ref, y_ref):
    o = jnp.concatenate([o_ref[0, h] for h in range(o_ref.shape[1])], axis=-1)
    y_ref[0] = x_ref[0] + _dot(o, w_ref[...])


def _wo_layer(o, x, w):
    B, S, D = x.shape
    tm = min(WO_TOKENS, S)
    tile = pl.BlockSpec((1, tm, D), lambda b_, i: (b_, i, 0))
    return pl.pallas_call(
        _wo_kernel,
        grid=(B, S // tm),
        in_specs=[pl.BlockSpec((1, N_HEADS, tm, HEAD_DIM), lambda b_, i: (b_, 0, i, 0)),
                  tile,
                  pl.BlockSpec((D, D), lambda b_, i: (0, 0), pipeline_mode=pl.Buffered(1))],
        out_specs=tile,
        out_shape=jax.ShapeDtypeStruct((B, S, D), F32),
        compiler_params=_params(("parallel", "parallel")),
        name="attn_out",
    )(o, x, w)


def _prep_weights(mix_norm, pool_w, pool_b, pool_scale, attn_w_qkv, attn_b_qkv, attn_rpb, attn_w_o,
                  ffn_norm, ffn_w_up, ffn_conv_w, ffn_conv_b, ffn_w_down, final_norm):
    assert pool_w.shape[0] == 1 and attn_w_qkv.shape[0] == 1 and ffn_w_up.shape[0] == 2
    q_scale = jnp.concatenate([jnp.full((D_MODEL,), HEAD_DIM ** -0.5 * LOG2_E, F32),
                               jnp.ones((2 * D_MODEL,), F32)]).reshape(1, 3 * D_MODEL)
    return dict(
        mix_norm=mix_norm.reshape(-1, 1, D_MODEL),
        pool_w=pool_w[0].astype(BF16),
        pool_b=pool_b[0].reshape(N_POOL_GROUPS, 1, GROUP_DIM),
        pool_scale=pool_scale[0].reshape(1, D_MODEL),
        w_qkv=attn_w_qkv[0].astype(BF16),
        b_qkv=attn_b_qkv[0].reshape(1, 3 * D_MODEL),
        q_scale=q_scale,
        bias=_attn_bias_tables(attn_rpb[0]),
        w_o=attn_w_o[0].astype(BF16),
        ffn=(ffn_norm.reshape(-1, 1, D_MODEL), ffn_w_up.astype(BF16), ffn_conv_w,
             ffn_conv_b.reshape(-1, 1, D_FF), ffn_w_down.astype(BF16)),
        final_norm=final_norm.reshape(1, D_MODEL),
    )


def _trunk(x, p):
    x = _pool_layer(x, p["mix_norm"][0], p["pool_w"], p["pool_b"], p["pool_scale"])
    x = _ffn_layer(x, 0, *p["ffn"], p["final_norm"], final_norm=False)
    qkv = _qkv_layer(x, p["mix_norm"][1], p["w_qkv"], p["b_qkv"], p["q_scale"])
    o = _attn_layer(qkv, p["bias"])
    x = _wo_layer(o, x, p["w_o"])
    return _ffn_layer(x, 1, *p["ffn"], p["final_norm"], final_norm=True)


def kernel(x_prompt, x_sample, mix_norm, pool_w, pool_b, pool_scale, attn_w_qkv, attn_b_qkv, attn_rpb,
           attn_w_o, ffn_norm, ffn_w_up, ffn_conv_w, ffn_conv_b, ffn_w_down, final_norm):
    p = _prep_weights(mix_norm, pool_w, pool_b, pool_scale, attn_w_qkv, attn_b_qkv, attn_rpb, attn_w_o,
                      ffn_norm, ffn_w_up, ffn_conv_w, ffn_conv_b, ffn_w_down, final_norm)
    return _trunk(x_prompt, p), _trunk(x_sample, p)
```

```python
import functools

import numpy as np
import jax
import jax.numpy as jnp
from jax import lax
from jax.experimental import pallas as pl
from jax.experimental.pallas import tpu as pltpu

D_MODEL = 2048
GRID_W = 64
N_POOL_GROUPS = 4
POOL_WINDOWS = (2, 4, 8, 16)
GROUP_DIM = D_MODEL // N_POOL_GROUPS
N_HEADS = 16
HEAD_DIM = D_MODEL // N_HEADS
WIN_H = 8
WIN_W = 16
D_FF = 5632
EPS = 1e-6
NEG_INF = -1e30
LOG2_E = float(np.log2(np.e))

HALO = 8
POOL_TOKENS = 512
FFN_TOKENS = 1024
FFN_CHUNK = 512
QKV_TOKENS = 1024
QKV_CHUNK = 2048
WO_TOKENS = 1024
ATT_ROWS = 32
ATT_HEADS = 2
ATT_KEY_ROWS = ATT_ROWS + WIN_H
ATT_SKEW = 4
VMEM_LIMIT = 60 * 1024 * 1024

F32 = jnp.float32
BF16 = jnp.bfloat16


def _rms(x, g):
    return x * lax.rsqrt(jnp.mean(x * x, axis=-1, keepdims=True) + EPS) * g


def _dot(a, b):
    return jnp.dot(a, b, preferred_element_type=F32)


def _params(sem):
    return pltpu.CompilerParams(dimension_semantics=sem, vmem_limit_bytes=VMEM_LIMIT)


def _halo_specs(tokens, seq):
    per = tokens // HALO
    last = seq // HALO - 1
    main = pl.BlockSpec((1, tokens, D_MODEL), lambda b, i, *_: (b, i, 0))
    prev = pl.BlockSpec((1, HALO, D_MODEL), lambda b, i, *_: (b, jnp.maximum(i * per - 1, 0), 0))
    nxt = pl.BlockSpec((1, HALO, D_MODEL), lambda b, i, *_: (b, jnp.minimum((i + 1) * per, last), 0))
    return main, prev, nxt


def _pool_kernel(xm_ref, xp_ref, xn_ref, g_ref, w_ref, b_ref, sc_ref, o_ref, *, seq, tp):
    i = pl.program_id(1)
    n = tp + 2 * HALO
    xm = xm_ref[0]
    g_all = g_ref[...]
    up = jnp.where(i > 0, _rms(xp_ref[0], g_all), 0.0)
    un = jnp.where(i < pl.num_programs(1) - 1, _rms(xn_ref[0], g_all), 0.0)
    u = jnp.concatenate([up, _rms(xm, g_all), un], axis=0)
    posm = i * tp + lax.broadcasted_iota(jnp.int32, (tp, 1), 0)
    for g, w in enumerate(POOL_WINDOWS):
        cols = slice(g * GROUP_DIM, (g + 1) * GROUP_DIM)
        ug = u[:, cols]
        a = ug + pltpu.roll(ug, 1, 0)
        width = 2
        while width < w:
            a = a + pltpu.roll(a, width, 0)
            width *= 2
        s = a if w == 2 else pltpu.roll(a, n - (w // 2 - 1), 0)
        lo = jnp.clip(posm - w // 2, 0, seq)
        hi = jnp.clip(posm - w // 2 + w, 0, seq)
        inv_cnt = 1.0 / (hi - lo).astype(F32)
        d = s[HALO:HALO + tp] * inv_cnt - ug[HALO:HALO + tp]
        y = _dot(d.astype(BF16), w_ref[g]) + b_ref[g]
        o_ref[0, :, cols] = xm[:, cols] + y * sc_ref[:, cols]


def _pool_layer(x, g, w, b, sc):
    B, S, D = x.shape
    tp = min(POOL_TOKENS, S)
    main, prev, nxt = _halo_specs(tp, S)
    const2 = lambda b_, i: (0, 0)
    const3 = lambda b_, i: (0, 0, 0)
    return pl.pallas_call(
        functools.partial(_pool_kernel, seq=S, tp=tp),
        grid=(B, S // tp),
        in_specs=[main, prev, nxt,
                  pl.BlockSpec((1, D), const2),
                  pl.BlockSpec((N_POOL_GROUPS, GROUP_DIM, GROUP_DIM), const3),
                  pl.BlockSpec((N_POOL_GROUPS, 1, GROUP_DIM), const3),
                  pl.BlockSpec((1, D), const2)],
        out_specs=pl.BlockSpec((1, tp, D), lambda b_, i: (b_, i, 0)),
        out_shape=jax.ShapeDtypeStruct((B, S, D), F32),
        compiler_params=_params(("parallel", "parallel")),
        name="pool_layer",
    )(x, x, x, g, w, b, sc)


def _gelu(x):
    return 0.5 * x * (1.0 + lax.erf(x * np.float32(np.sqrt(0.5))))


def _ffn_kernel(x_hbm, nw_ref, wg_ref, wv_ref, cw_ref, cb_ref, wd_ref, fn_ref,
                o_ref, xbuf, u_ref, sem, *, seq, tm, final_norm):
    b, i, j = pl.program_id(0), pl.program_id(1), pl.program_id(2)
    ni = pl.num_programs(1)
    n_tiles = pl.num_programs(0) * ni
    t = b * ni + i
    n = tm + 2 * HALO
    last = pl.num_programs(2) - 1

    def tile_copies(tile):
        bt, it = tile // ni, tile % ni
        pieces = ((jnp.maximum(it * tm - HALO, 0), 0, HALO),
                  (it * tm, HALO, tm),
                  (jnp.minimum((it + 1) * tm, seq - HALO), HALO + tm, HALO))
        return [pltpu.make_async_copy(x_hbm.at[bt, pl.ds(pl.multiple_of(src, HALO), size)],
                                      xbuf.at[pl.ds(dst, size)], sem.at[k])
                for k, (src, dst, size) in enumerate(pieces)]

    @pl.when(j == 0)
    def _():
        @pl.when(t == 0)
        def _():
            for c in tile_copies(0):
                c.start()

        for c in tile_copies(t):
            c.wait()
        xm = xbuf[HALO:HALO + tm]
        g = nw_ref[0]
        u_ref[0:tm] = _rms(xm, g).astype(BF16)
        un = jnp.where(i < ni - 1, _rms(xbuf[HALO + tm:n], g), 0.0)
        up = jnp.where(i > 0, _rms(xbuf[0:HALO], g), 0.0)
        u_ref[tm:n] = jnp.concatenate([un, up], axis=0).astype(BF16)
        o_ref[0] = xm

    @pl.when((j == 1) & (t + 1 < n_tiles))
    def _():
        for c in tile_copies(t + 1):
            c.start()

    def chunk(c):
        cs = slice(c * FFN_CHUNK, (c + 1) * FFN_CHUNK)
        gate = _dot(u_ref[...], wg_ref[0, :, cs])
        val = _dot(u_ref[0:tm], wv_ref[0, :, cs])
        cw = cw_ref[0, :, cs]
        conv = (pltpu.roll(gate, 1, 0)[:tm] * cw[0:1] + gate[:tm] * cw[1:2]
                + pltpu.roll(gate, n - 1, 0)[:tm] * cw[2:3] + cb_ref[0, :, cs])
        h = _gelu(conv) * val
        o_ref[0] += _dot(h.astype(BF16), wd_ref[0, cs, :])

    @pl.when(j < last)
    def _():
        chunk(0)
        chunk(1)

    @pl.when(j == last)
    def _():
        chunk(0)
        if final_norm:
            o_ref[0] = _rms(o_ref[0], fn_ref[...])


def _ffn_layer(x, layer, nw, wg, wv, cw, cb, w_down, fn, *, final_norm):
    B, S, D = x.shape
    tm = min(FFN_TOKENS, S)
    tf = 2 * FFN_CHUNK
    nj = pl.cdiv(D_FF, tf)
    assert D_FF % FFN_CHUNK == 0 and (D_FF // FFN_CHUNK) % 2 == 1 and nj >= 2 and S % tm == 0
    return pl.pallas_call(
        functools.partial(_ffn_kernel, seq=S, tm=tm, final_norm=final_norm),
        grid=(B, S // tm, nj),
        in_specs=[pl.BlockSpec(memory_space=pl.ANY),
                  pl.BlockSpec((1, 1, D), lambda b_, i, j: (layer, 0, 0)),
                  pl.BlockSpec((1, D, tf), lambda b_, i, j: (layer, 0, j)),
                  pl.BlockSpec((1, D, tf), lambda b_, i, j: (layer, 0, j)),
                  pl.BlockSpec((1, 3, tf), lambda b_, i, j: (layer, 0, j)),
                  pl.BlockSpec((1, 1, tf), lambda b_, i, j: (layer, 0, j)),
                  pl.BlockSpec((1, tf, D), lambda b_, i, j: (layer, j, 0)),
                  pl.BlockSpec((1, D), lambda b_, i, j: (0, 0))],
        out_specs=pl.BlockSpec((1, tm, D), lambda b_, i, j: (b_, i, 0)),
        out_shape=jax.ShapeDtypeStruct((B, S, D), F32),
        scratch_shapes=[pltpu.VMEM((tm + 2 * HALO, D), F32),
                        pltpu.VMEM((tm + 2 * HALO, D), BF16),
                        pltpu.SemaphoreType.DMA((3,))],
        compiler_params=_params(("arbitrary", "arbitrary", "arbitrary")),
        name="ffn_final" if final_norm else "ffn",
    )(x, nw, wg, wv, cw, cb, w_down, fn)


def _qkv_kernel(x_ref, nw_ref, w_ref, b_ref, s_ref, o_ref, u_ref):
    @pl.when(pl.program_id(2) == 0)
    def _():
        u_ref[...] = _rms(x_ref[0], nw_ref[...]).astype(BF16)

    res = ((_dot(u_ref[...], w_ref[...]) + b_ref[...]) * s_ref[...]).astype(BF16)
    for hh in range(o_ref.shape[1]):
        o_ref[0, hh] = res[:, hh * HEAD_DIM:(hh + 1) * HEAD_DIM]


def _qkv_layer(x, nw, w, b, s):
    B, S, D = x.shape
    tm = min(QKV_TOKENS, S)
    tn = QKV_CHUNK
    N = w.shape[1]
    return pl.pallas_call(
        _qkv_kernel,
        grid=(B, S // tm, N // tn),
        in_specs=[pl.BlockSpec((1, tm, D), lambda b_, i, j: (b_, i, 0)),
                  pl.BlockSpec((1, D), lambda b_, i, j: (0, 0)),
                  pl.BlockSpec((D, tn), lambda b_, i, j: (0, j)),
                  pl.BlockSpec((1, tn), lambda b_, i, j: (0, j)),
                  pl.BlockSpec((1, tn), lambda b_, i, j: (0, j))],
        out_specs=pl.BlockSpec((1, tn // HEAD_DIM, tm, HEAD_DIM), lambda b_, i, j: (b_, j, i, 0)),
        out_shape=jax.ShapeDtypeStruct((B, N // HEAD_DIM, S, HEAD_DIM), BF16),
        scratch_shapes=[pltpu.VMEM((tm, D), BF16)],
        compiler_params=_params(("parallel", "parallel", "arbitrary")),
        name="qkv",
    )(x, nw, w, b, s)


def _row_patterns():
    rows = 4 * ATT_ROWS
    kstart, tid = np.zeros((3, ATT_ROWS), int), np.zeros((3, ATT_ROWS), int)
    for v, r0 in enumerate((0, ATT_ROWS, rows - ATT_ROWS)):
        kb = np.clip(r0 - WIN_H // 2, 0, rows - ATT_KEY_ROWS)
        r = r0 + np.arange(ATT_ROWS)
        rs = np.clip(r - WIN_H // 2, 0, rows - WIN_H)
        kstart[v] = rs - kb
        tid[v] = rs - r + WIN_H - 1
    assert kstart.min() >= 0 and kstart.max() + WIN_H <= ATT_KEY_ROWS
    assert tid.min() >= 0 and tid.max() < WIN_H
    return kstart, tid


_ROW_KSTART, _ROW_TID = _row_patterns()


def _attn_bias_tables(rpb):
    H = rpb.shape[0]
    c = np.arange(GRID_W)
    cstart = np.clip(c - WIN_W // 2, 0, GRID_W - WIN_W)
    col_ok = (c[None, :] >= cstart[:, None]) & (c[None, :] < cstart[:, None] + WIN_W)
    span = 2 * GRID_W
    vi = np.clip(np.arange(span) - (GRID_W - 1), -(WIN_W - 1), WIN_W - 1) + WIN_W - 1
    vp = rpb[:, :, vi]
    skew = jnp.broadcast_to(vp[:, :, None, :], (H, 2 * WIN_H - 1, GRID_W, span))
    skew = skew.reshape(H, 2 * WIN_H - 1, GRID_W * span)[:, :, :GRID_W * (span - 1)]
    t = skew.reshape(H, 2 * WIN_H - 1, GRID_W, span - 1)[..., GRID_W - 1:]
    t = jnp.where(col_ok, t * LOG2_E, NEG_INF)
    t = t.transpose(0, 2, 1, 3).reshape(H, GRID_W, (2 * WIN_H - 1) * GRID_W)
    return jnp.stack([t[:, :, s * GRID_W:(s + WIN_H) * GRID_W] for s in range(WIN_H)], axis=1)


def _attn_kernel(q_ref, k_ref, v_ref, bias_ref, o_ref):
    i = pl.program_id(2)
    first = i == 0
    last = i == pl.num_programs(2) - 1
    nq = GRID_W
    nk = WIN_H * GRID_W

    def pick(tab, r):
        return jnp.where(first, int(tab[0, r]), jnp.where(last, int(tab[2, r]), int(tab[1, r])))

    def scores(hh, r):
        off = pl.multiple_of(pick(_ROW_KSTART, r) * GRID_W, GRID_W)
        q = q_ref[0, hh, r * nq:(r + 1) * nq, :]
        s = lax.dot_general(q, k_ref[0, hh, pl.ds(off, nk), :], (((1,), (1,)), ((), ())),
                            preferred_element_type=F32)
        return s + bias_ref[hh, pick(_ROW_TID, r)]

    def softmax(s):
        p = jnp.exp2(s - jnp.max(s, axis=-1, keepdims=True))
        return p.astype(BF16), jnp.sum(p, axis=-1, keepdims=True)

    def weighted(hh, r, p, l):
        off = pl.multiple_of(pick(_ROW_KSTART, r) * GRID_W, GRID_W)
        o = _dot(p, v_ref[0, hh, pl.ds(off, nk), :]) / l
        o_ref[0, hh, r * nq:(r + 1) * nq, :] = o.astype(BF16)

    work = [(hh, r) for hh in range(q_ref.shape[1]) for r in range(ATT_ROWS)]
    s_live, p_live = {}, {}
    d1, d2 = ATT_SKEW, 2 * ATT_SKEW
    for t in range(len(work) + d2):
        if t < len(work):
            s_live[t] = scores(*work[t])
        if 0 <= t - d1 < len(work):
            p_live[t - d1] = softmax(s_live.pop(t - d1))
        if 0 <= t - d2 < len(work):
            weighted(*work[t - d2], *p_live.pop(t - d2))


def _attn_layer(qkv, bias):
    B, _, S, _ = qkv.shape
    H, hb = N_HEADS, ATT_HEADS
    tq = ATT_ROWS * GRID_W
    tk = ATT_KEY_ROWS * GRID_W
    halo = (tk - tq) // 2
    nblk = S // tq
    assert nblk >= 3 and S % tq == 0 and tq % halo == 0
    ng = H // hb

    def keys(part):
        return pl.BlockSpec(
            (pl.Element(1), pl.Element(hb), pl.Element(tk), pl.Element(HEAD_DIM)),
            lambda b_, g, i: (b_, (part * ng + g) * hb,
                              halo * jnp.clip((tq // halo) * i - 1, 0, (S - tk) // halo), 0))

    nt = bias.shape[1]
    return pl.pallas_call(
        _attn_kernel,
        grid=(B, ng, nblk),
        in_specs=[pl.BlockSpec((1, hb, tq, HEAD_DIM), lambda b_, g, i: (b_, g, i, 0)),
                  keys(1), keys(2),
                  pl.BlockSpec((hb, nt, GRID_W, WIN_H * GRID_W), lambda b_, g, i: (g, 0, 0, 0))],
        out_specs=pl.BlockSpec((1, hb, tq, HEAD_DIM), lambda b_, g, i: (b_, g, i, 0)),
        out_shape=jax.ShapeDtypeStruct((B, H, S, HEAD_DIM), BF16),
        compiler_params=_params(("parallel", "parallel", "arbitrary")),
        name="natten",
    )(qkv, qkv, qkv, bias)


def _wo_kernel(o_ref, x_ref, w_ref, y_ref):
    o = jnp.concatenate([o_ref[0, h] for h in range(o_ref.shape[1])], axis=-1)
    y_ref[0] = x_ref[0] + _dot(o, w_ref[...])


def _wo_layer(o, x, w):
    B, S, D = x.shape
    tm = min(WO_TOKENS, S)
    tile = pl.BlockSpec((1, tm, D), lambda b_, i: (b_, i, 0))
    return pl.pallas_call(
        _wo_kernel,
        grid=(B, S // tm),
        in_specs=[pl.BlockSpec((1, N_HEADS, tm, HEAD_DIM), lambda b_, i: (b_, 0, i, 0)),
                  tile,
                  pl.BlockSpec((D, D), lambda b_, i: (0, 0), pipeline_mode=pl.Buffered(1))],
        out_specs=tile,
        out_shape=jax.ShapeDtypeStruct((B, S, D), F32),
        compiler_params=_params(("parallel", "parallel")),
        name="attn_out",
    )(o, x, w)


def _prep_weights(mix_norm, pool_w, pool_b, pool_scale, attn_w_qkv, attn_b_qkv, attn_rpb, attn_w_o,
                  ffn_norm, ffn_w_up, ffn_conv_w, ffn_conv_b, ffn_w_down, final_norm):
    assert pool_w.shape[0] == 1 and attn_w_qkv.shape[0] == 1 and ffn_w_up.shape[0] == 2
    q_scale = jnp.concatenate([jnp.full((D_MODEL,), HEAD_DIM ** -0.5 * LOG2_E, F32),
                               jnp.ones((2 * D_MODEL,), F32)]).reshape(1, 3 * D_MODEL)
    return dict(
        mix_norm=mix_norm.reshape(-1, 1, D_MODEL),
        pool_w=pool_w[0].astype(BF16),
        pool_b=pool_b[0].reshape(N_POOL_GROUPS, 1, GROUP_DIM),
        pool_scale=pool_scale[0].reshape(1, D_MODEL),
        w_qkv=attn_w_qkv[0].astype(BF16),
        b_qkv=attn_b_qkv[0].reshape(1, 3 * D_MODEL),
        q_scale=q_scale,
        bias=_attn_bias_tables(attn_rpb[0]),
        w_o=attn_w_o[0].astype(BF16),
        ffn=(ffn_norm.reshape(-1, 1, D_MODEL), ffn_w_up[:, :, :D_FF].astype(BF16),
             ffn_w_up[:, :, D_FF:].astype(BF16), ffn_conv_w, ffn_conv_b.reshape(-1, 1, D_FF),
             ffn_w_down.astype(BF16)),
        final_norm=final_norm.reshape(1, D_MODEL),
    )


def _trunk(x, p):
    x = _pool_layer(x, p["mix_norm"][0], p["pool_w"], p["pool_b"], p["pool_scale"])
    x = _ffn_layer(x, 0, *p["ffn"], p["final_norm"], final_norm=False)
    qkv = _qkv_layer(x, p["mix_norm"][1], p["w_qkv"], p["b_qkv"], p["q_scale"])
    o = _attn_layer(qkv, p["bias"])
    x = _wo_layer(o, x, p["w_o"])
    return _ffn_layer(x, 1, *p["ffn"], p["final_norm"], final_norm=True)


def kernel(x_prompt, x_sample, mix_norm, pool_w, pool_b, pool_scale, attn_w_qkv, attn_b_qkv, attn_rpb,
           attn_w_o, ffn_norm, ffn_w_up, ffn_conv_w, ffn_conv_b, ffn_w_down, final_norm):
    p = _prep_weights(mix_norm, pool_w, pool_b, pool_scale, attn_w_qkv, attn_b_qkv, attn_rpb, attn_w_o,
                      ffn_norm, ffn_w_up, ffn_conv_w, ffn_conv_b, ffn_w_down, final_norm)
    return _trunk(x_prompt, p), _trunk(x_sample, p)
```

```python
import functools

import numpy as np
import jax
import jax.numpy as jnp
from jax import lax
from jax.experimental import pallas as pl
from jax.experimental.pallas import tpu as pltpu

D_MODEL = 2048
GRID_W = 64
N_POOL_GROUPS = 4
POOL_WINDOWS = (2, 4, 8, 16)
GROUP_DIM = D_MODEL // N_POOL_GROUPS
N_HEADS = 16
HEAD_DIM = D_MODEL // N_HEADS
WIN_H = 8
WIN_W = 16
D_FF = 5632
EPS = 1e-6
NEG_INF = -1e30
LOG2_E = float(np.log2(np.e))

HALO = 8
POOL_TOKENS = 512
FFN_TOKENS = 1024
FFN_CHUNK = 512
QKV_TOKENS = 1024
QKV_CHUNK = 2048
WO_TOKENS = 1024
ATT_ROWS = 32
ATT_HEADS = 2
ATT_KEY_ROWS = ATT_ROWS + WIN_H
ATT_SKEW = 4
VMEM_LIMIT = 60 * 1024 * 1024

F32 = jnp.float32
BF16 = jnp.bfloat16


def _rms(x, g):
    return x * lax.rsqrt(jnp.mean(x * x, axis=-1, keepdims=True) + EPS) * g


def _dot(a, b):
    return jnp.dot(a, b, preferred_element_type=F32)


def _params(sem):
    return pltpu.CompilerParams(dimension_semantics=sem, vmem_limit_bytes=VMEM_LIMIT)


def _halo_specs(tokens, seq):
    per = tokens // HALO
    last = seq // HALO - 1
    main = pl.BlockSpec((1, tokens, D_MODEL), lambda b, i, *_: (b, i, 0))
    prev = pl.BlockSpec((1, HALO, D_MODEL), lambda b, i, *_: (b, jnp.maximum(i * per - 1, 0), 0))
    nxt = pl.BlockSpec((1, HALO, D_MODEL), lambda b, i, *_: (b, jnp.minimum((i + 1) * per, last), 0))
    return main, prev, nxt


def _pool_kernel(xm_ref, xp_ref, xn_ref, g_ref, w_ref, b_ref, sc_ref, o_ref, *, seq, tp):
    i = pl.program_id(1)
    n = tp + 2 * HALO
    xm = xm_ref[0]
    g_all = g_ref[...]
    up = jnp.where(i > 0, _rms(xp_ref[0], g_all), 0.0)
    un = jnp.where(i < pl.num_programs(1) - 1, _rms(xn_ref[0], g_all), 0.0)
    u = jnp.concatenate([up, _rms(xm, g_all), un], axis=0)
    posm = i * tp + lax.broadcasted_iota(jnp.int32, (tp, 1), 0)
    for g, w in enumerate(POOL_WINDOWS):
        cols = slice(g * GROUP_DIM, (g + 1) * GROUP_DIM)
        ug = u[:, cols]
        a = ug + pltpu.roll(ug, 1, 0)
        width = 2
        while width < w:
            a = a + pltpu.roll(a, width, 0)
            width *= 2
        s = a if w == 2 else pltpu.roll(a, n - (w // 2 - 1), 0)
        lo = jnp.clip(posm - w // 2, 0, seq)
        hi = jnp.clip(posm - w // 2 + w, 0, seq)
        inv_cnt = 1.0 / (hi - lo).astype(F32)
        d = s[HALO:HALO + tp] * inv_cnt - ug[HALO:HALO + tp]
        y = _dot(d.astype(BF16), w_ref[g]) + b_ref[g]
        o_ref[0, :, cols] = xm[:, cols] + y * sc_ref[:, cols]


def _pool_layer(x, g, w, b, sc):
    B, S, D = x.shape
    tp = min(POOL_TOKENS, S)
    main, prev, nxt = _halo_specs(tp, S)
    const2 = lambda b_, i: (0, 0)
    const3 = lambda b_, i: (0, 0, 0)
    return pl.pallas_call(
        functools.partial(_pool_kernel, seq=S, tp=tp),
        grid=(B, S // tp),
        in_specs=[main, prev, nxt,
                  pl.BlockSpec((1, D), const2),
                  pl.BlockSpec((N_POOL_GROUPS, GROUP_DIM, GROUP_DIM), const3),
                  pl.BlockSpec((N_POOL_GROUPS, 1, GROUP_DIM), const3),
                  pl.BlockSpec((1, D), const2)],
        out_specs=pl.BlockSpec((1, tp, D), lambda b_, i: (b_, i, 0)),
        out_shape=jax.ShapeDtypeStruct((B, S, D), F32),
        compiler_params=_params(("parallel", "parallel")),
        name="pool_layer",
    )(x, x, x, g, w, b, sc)


def _gelu(x):
    return 0.5 * x * (1.0 + lax.erf(x * np.float32(np.sqrt(0.5))))


def _ffn_kernel(x_hbm, nw_ref, wg_ref, wv_ref, cw_ref, cb_ref, wd_ref, fn_ref,
                o_ref, xbuf, u_ref, sem, *, seq, tm, final_norm):
    b, i, j = pl.program_id(0), pl.program_id(1), pl.program_id(2)
    ni = pl.num_programs(1)
    n_tiles = pl.num_programs(0) * ni
    t = b * ni + i
    n = tm + 2 * HALO
    last = pl.num_programs(2) - 1

    def tile_copies(tile):
        bt, it = tile // ni, tile % ni
        pieces = ((jnp.maximum(it * tm - HALO, 0), 0, HALO),
                  (it * tm, HALO, tm),
                  (jnp.minimum((it + 1) * tm, seq - HALO), HALO + tm, HALO))
        return [pltpu.make_async_copy(x_hbm.at[bt, pl.ds(pl.multiple_of(src, HALO), size)],
                                      xbuf.at[pl.ds(dst, size)], sem.at[k])
                for k, (src, dst, size) in enumerate(pieces)]

    def chunk(c):
        cs = slice(c * FFN_CHUNK, (c + 1) * FFN_CHUNK)
        gate = _dot(u_ref[...], wg_ref[0, :, cs])
        val = _dot(u_ref[0:tm], wv_ref[0, :, cs])
        cw = cw_ref[0, :, cs]
        conv = (pltpu.roll(gate, 1, 0)[:tm] * cw[0:1] + gate[:tm] * cw[1:2]
                + pltpu.roll(gate, n - 1, 0)[:tm] * cw[2:3] + cb_ref[0, :, cs])
        h = _gelu(conv) * val
        o_ref[0] += _dot(h.astype(BF16), wd_ref[0, cs, :])

    @pl.when(j == 0)
    def _():
        @pl.when(t == 0)
        def _():
            for c in tile_copies(0):
                c.start()

        for c in tile_copies(t):
            c.wait()
        xm = xbuf[HALO:HALO + tm]
        g = nw_ref[0]
        u_ref[0:tm] = _rms(xm, g).astype(BF16)
        un = jnp.where(i < ni - 1, _rms(xbuf[HALO + tm:n], g), 0.0)
        up = jnp.where(i > 0, _rms(xbuf[0:HALO], g), 0.0)
        u_ref[tm:n] = jnp.concatenate([un, up], axis=0).astype(BF16)
        o_ref[0] = xm
        chunk(0)
        chunk(1)

    @pl.when((j == 1) & (t + 1 < n_tiles))
    def _():
        for c in tile_copies(t + 1):
            c.start()

    @pl.when((j > 0) & (j < last))
    def _():
        chunk(0)
        chunk(1)

    @pl.when(j == last)
    def _():
        chunk(0)
        if final_norm:
            o_ref[0] = _rms(o_ref[0], fn_ref[...])


def _ffn_layer(x, layer, nw, wg, wv, cw, cb, w_down, fn, *, final_norm):
    B, S, D = x.shape
    tm = min(FFN_TOKENS, S)
    tf = 2 * FFN_CHUNK
    nj = pl.cdiv(D_FF, tf)
    assert D_FF % FFN_CHUNK == 0 and (D_FF // FFN_CHUNK) % 2 == 1 and nj >= 2 and S % tm == 0
    return pl.pallas_call(
        functools.partial(_ffn_kernel, seq=S, tm=tm, final_norm=final_norm),
        grid=(B, S // tm, nj),
        in_specs=[pl.BlockSpec(memory_space=pl.ANY),
                  pl.BlockSpec((1, 1, D), lambda b_, i, j: (layer, 0, 0)),
                  pl.BlockSpec((1, D, tf), lambda b_, i, j: (layer, 0, j)),
                  pl.BlockSpec((1, D, tf), lambda b_, i, j: (layer, 0, j)),
                  pl.BlockSpec((1, 3, tf), lambda b_, i, j: (layer, 0, j)),
                  pl.BlockSpec((1, 1, tf), lambda b_, i, j: (layer, 0, j)),
                  pl.BlockSpec((1, tf, D), lambda b_, i, j: (layer, j, 0)),
                  pl.BlockSpec((1, D), lambda b_, i, j: (0, 0))],
        out_specs=pl.BlockSpec((1, tm, D), lambda b_, i, j: (b_, i, 0)),
        out_shape=jax.ShapeDtypeStruct((B, S, D), F32),
        scratch_shapes=[pltpu.VMEM((tm + 2 * HALO, D), F32),
                        pltpu.VMEM((tm + 2 * HALO, D), BF16),
                        pltpu.SemaphoreType.DMA((3,))],
        compiler_params=_params(("arbitrary", "arbitrary", "arbitrary")),
        name="ffn_final" if final_norm else "ffn",
    )(x, nw, wg, wv, cw, cb, w_down, fn)


def _qkv_kernel(x_ref, nw_ref, w_ref, b_ref, s_ref, o_ref, u_ref):
    def project():
        res = ((_dot(u_ref[...], w_ref[...]) + b_ref[...]) * s_ref[...]).astype(BF16)
        for hh in range(o_ref.shape[1]):
            o_ref[0, hh] = res[:, hh * HEAD_DIM:(hh + 1) * HEAD_DIM]

    @pl.when(pl.program_id(2) == 0)
    def _():
        u_ref[...] = _rms(x_ref[0], nw_ref[...]).astype(BF16)
        project()

    @pl.when(pl.program_id(2) > 0)
    def _():
        project()


def _qkv_layer(x, nw, w, b, s):
    B, S, D = x.shape
    tm = min(QKV_TOKENS, S)
    tn = QKV_CHUNK
    N = w.shape[1]
    return pl.pallas_call(
        _qkv_kernel,
        grid=(B, S // tm, N // tn),
        in_specs=[pl.BlockSpec((1, tm, D), lambda b_, i, j: (b_, i, 0)),
                  pl.BlockSpec((1, D), lambda b_, i, j: (0, 0)),
                  pl.BlockSpec((D, tn), lambda b_, i, j: (0, j)),
                  pl.BlockSpec((1, tn), lambda b_, i, j: (0, j)),
                  pl.BlockSpec((1, tn), lambda b_, i, j: (0, j))],
        out_specs=pl.BlockSpec((1, tn // HEAD_DIM, tm, HEAD_DIM), lambda b_, i, j: (b_, j, i, 0)),
        out_shape=jax.ShapeDtypeStruct((B, N // HEAD_DIM, S, HEAD_DIM), BF16),
        scratch_shapes=[pltpu.VMEM((tm, D), BF16)],
        compiler_params=_params(("parallel", "parallel", "arbitrary")),
        name="qkv",
    )(x, nw, w, b, s)


def _row_patterns():
    rows = 4 * ATT_ROWS
    kstart, tid = np.zeros((3, ATT_ROWS), int), np.zeros((3, ATT_ROWS), int)
    for v, r0 in enumerate((0, ATT_ROWS, rows - ATT_ROWS)):
        kb = np.clip(r0 - WIN_H // 2, 0, rows - ATT_KEY_ROWS)
        r = r0 + np.arange(ATT_ROWS)
        rs = np.clip(r - WIN_H // 2, 0, rows - WIN_H)
        kstart[v] = rs - kb
        tid[v] = rs - r + WIN_H - 1
    assert kstart.min() >= 0 and kstart.max() + WIN_H <= ATT_KEY_ROWS
    assert tid.min() >= 0 and tid.max() < WIN_H
    return kstart, tid


_ROW_KSTART, _ROW_TID = _row_patterns()


def _attn_bias_tables(rpb):
    H = rpb.shape[0]
    c = np.arange(GRID_W)
    cstart = np.clip(c - WIN_W // 2, 0, GRID_W - WIN_W)
    col_ok = (c[None, :] >= cstart[:, None]) & (c[None, :] < cstart[:, None] + WIN_W)
    span = 2 * GRID_W
    vi = np.clip(np.arange(span) - (GRID_W - 1), -(WIN_W - 1), WIN_W - 1) + WIN_W - 1
    vp = rpb[:, :, vi]
    skew = jnp.broadcast_to(vp[:, :, None, :], (H, 2 * WIN_H - 1, GRID_W, span))
    skew = skew.reshape(H, 2 * WIN_H - 1, GRID_W * span)[:, :, :GRID_W * (span - 1)]
    t = skew.reshape(H, 2 * WIN_H - 1, GRID_W, span - 1)[..., GRID_W - 1:]
    t = jnp.where(col_ok, t * LOG2_E, NEG_INF)
    t = t.transpose(0, 2, 1, 3).reshape(H, GRID_W, (2 * WIN_H - 1) * GRID_W)
    return jnp.stack([t[:, :, s * GRID_W:(s + WIN_H) * GRID_W] for s in range(WIN_H)], axis=1)


def _attn_kernel(q_ref, k_ref, v_ref, bias_ref, o_ref):
    i = pl.program_id(2)
    first = i == 0
    last = i == pl.num_programs(2) - 1
    nq = GRID_W
    nk = WIN_H * GRID_W

    def pick(tab, r):
        return jnp.where(first, int(tab[0, r]), jnp.where(last, int(tab[2, r]), int(tab[1, r])))

    def scores(hh, r):
        off = pl.multiple_of(pick(_ROW_KSTART, r) * GRID_W, GRID_W)
        q = q_ref[0, hh, r * nq:(r + 1) * nq, :]
        s = lax.dot_general(q, k_ref[0, hh, pl.ds(off, nk), :], (((1,), (1,)), ((), ())),
                            preferred_element_type=F32)
        return s + bias_ref[hh, pick(_ROW_TID, r)]

    def softmax(s):
        p = jnp.exp2(s - jnp.max(s, axis=-1, keepdims=True))
        return p.astype(BF16), jnp.sum(p, axis=-1, keepdims=True)

    def weighted(hh, r, p, l):
        off = pl.multiple_of(pick(_ROW_KSTART, r) * GRID_W, GRID_W)
        o = _dot(p, v_ref[0, hh, pl.ds(off, nk), :]) / l
        o_ref[0, hh, r * nq:(r + 1) * nq, :] = o.astype(BF16)

    work = [(hh, r) for hh in range(q_ref.shape[1]) for r in range(ATT_ROWS)]
    s_live, p_live = {}, {}
    d1, d2 = ATT_SKEW, 2 * ATT_SKEW
    for t in range(len(work) + d2):
        if t < len(work):
            s_live[t] = scores(*work[t])
        if 0 <= t - d1 < len(work):
            p_live[t - d1] = softmax(s_live.pop(t - d1))
        if 0 <= t - d2 < len(work):
            weighted(*work[t - d2], *p_live.pop(t - d2))


def _attn_layer(qkv, bias):
    B, _, S, _ = qkv.shape
    H, hb = N_HEADS, ATT_HEADS
    tq = ATT_ROWS * GRID_W
    tk = ATT_KEY_ROWS * GRID_W
    halo = (tk - tq) // 2
    nblk = S // tq
    assert nblk >= 3 and S % tq == 0 and tq % halo == 0
    ng = H // hb

    def keys(part):
        return pl.BlockSpec(
            (pl.Element(1), pl.Element(hb), pl.Element(tk), pl.Element(HEAD_DIM)),
            lambda b_, g, i: (b_, (part * ng + g) * hb,
                              halo * jnp.clip((tq // halo) * i - 1, 0, (S - tk) // halo), 0))

    nt = bias.shape[1]
    return pl.pallas_call(
        _attn_kernel,
        grid=(B, ng, nblk),
        in_specs=[pl.BlockSpec((1, hb, tq, HEAD_DIM), lambda b_, g, i: (b_, g, i, 0)),
                  keys(1), keys(2),
                  pl.BlockSpec((hb, nt, GRID_W, WIN_H * GRID_W), lambda b_, g, i: (g, 0, 0, 0))],
        out_specs=pl.BlockSpec((1, hb, tq, HEAD_DIM), lambda b_, g, i: (b_, g, i, 0)),
        out_shape=jax.ShapeDtypeStruct((B, H, S, HEAD_DIM), BF16),
        compiler_params=_params(("parallel", "parallel", "arbitrary")),
        name="natten",
    )(qkv, qkv, qkv, bias)


def _wo_kernel(o_ref, x_ref, w_ref, y_ref):
    o = jnp.concatenate([o_ref[0, h] for h in range(o_ref.shape[1])], axis=-1)
    y_ref[0] = x_ref[0] + _dot(o, w_ref[...])


def _wo_layer(o, x, w):
    B, S, D = x.shape
    tm = min(WO_TOKENS, S)
    tile = pl.BlockSpec((1, tm, D), lambda b_, i: (b_, i, 0))
    return pl.pallas_call(
        _wo_kernel,
        grid=(B, S // tm),
        in_specs=[pl.BlockSpec((1, N_HEADS, tm, HEAD_DIM), lambda b_, i: (b_, 0, i, 0)),
                  tile,
                  pl.BlockSpec((D, D), lambda b_, i: (0, 0), pipeline_mode=pl.Buffered(1))],
        out_specs=tile,
        out_shape=jax.ShapeDtypeStruct((B, S, D), F32),
        compiler_params=_params(("parallel", "parallel")),
        name="attn_out",
    )(o, x, w)


def _prep_weights(mix_norm, pool_w, pool_b, pool_scale, attn_w_qkv, attn_b_qkv, attn_rpb, attn_w_o,
                  ffn_norm, ffn_w_up, ffn_conv_w, ffn_conv_b, ffn_w_down, final_norm):
    assert pool_w.shape[0] == 1 and attn_w_qkv.shape[0] == 1 and ffn_w_up.shape[0] == 2
    q_scale = jnp.concatenate([jnp.full((D_MODEL,), HEAD_DIM ** -0.5 * LOG2_E, F32),
                               jnp.ones((2 * D_MODEL,), F32)]).reshape(1, 3 * D_MODEL)
    return dict(
        mix_norm=mix_norm.reshape(-1, 1, D_MODEL),
        pool_w=pool_w[0].astype(BF16),
        pool_b=pool_b[0].reshape(N_POOL_GROUPS, 1, GROUP_DIM),
        pool_scale=pool_scale[0].reshape(1, D_MODEL),
        w_qkv=attn_w_qkv[0].astype(BF16),
        b_qkv=attn_b_qkv[0].reshape(1, 3 * D_MODEL),
        q_scale=q_scale,
        bias=_attn_bias_tables(attn_rpb[0]),
        w_o=attn_w_o[0].astype(BF16),
        ffn=(ffn_norm.reshape(-1, 1, D_MODEL), ffn_w_up[:, :, :D_FF].astype(BF16),
             ffn_w_up[:, :, D_FF:].astype(BF16), ffn_conv_w, ffn_conv_b.reshape(-1, 1, D_FF),
             ffn_w_down.astype(BF16)),
        final_norm=final_norm.reshape(1, D_MODEL),
    )


def _trunk(x, p):
    x = _pool_layer(x, p["mix_norm"][0], p["pool_w"], p["pool_b"], p["pool_scale"])
    x = _ffn_layer(x, 0, *p["ffn"], p["final_norm"], final_norm=False)
    qkv = _qkv_layer(x, p["mix_norm"][1], p["w_qkv"], p["b_qkv"], p["q_scale"])
    o = _attn_layer(qkv, p["bias"])
    x = _wo_layer(o, x, p["w_o"])
    return _ffn_layer(x, 1, *p["ffn"], p["final_norm"], final_norm=True)


def kernel(x_prompt, x_sample, mix_norm, pool_w, pool_b, pool_scale, attn_w_qkv, attn_b_qkv, attn_rpb,
           attn_w_o, ffn_norm, ffn_w_up, ffn_conv_w, ffn_conv_b, ffn_w_down, final_norm):
    p = _prep_weights(mix_norm, pool_w, pool_b, pool_scale, attn_w_qkv, attn_b_qkv, attn_rpb, attn_w_o,
                      ffn_norm, ffn_w_up, ffn_conv_w, ffn_conv_b, ffn_w_down, final_norm)
    return _trunk(x_prompt, p), _trunk(x_sample, p)
```

```python
import functools

import numpy as np
import jax
import jax.numpy as jnp
from jax import lax
from jax.experimental import pallas as pl
from jax.experimental.pallas import tpu as pltpu

D_MODEL = 2048
GRID_W = 64
N_POOL_GROUPS = 4
POOL_WINDOWS = (2, 4, 8, 16)
GROUP_DIM = D_MODEL // N_POOL_GROUPS
N_HEADS = 16
HEAD_DIM = D_MODEL // N_HEADS
WIN_H = 8
WIN_W = 16
D_FF = 5632
EPS = 1e-6
NEG_INF = -1e30
LOG2_E = float(np.log2(np.e))

HALO = 8
POOL_TOKENS = 512
FFN_TOKENS = 1024
FFN_CHUNK = 512
QKV_TOKENS = 1024
QKV_CHUNK = 2048
WO_TOKENS = 1024
ATT_ROWS = 32
ATT_HEADS = 2
ATT_KEY_ROWS = ATT_ROWS + WIN_H
ATT_SKEW = 4
VMEM_LIMIT = 60 * 1024 * 1024

F32 = jnp.float32
BF16 = jnp.bfloat16


def _rms(x, g):
    return x * lax.rsqrt(jnp.mean(x * x, axis=-1, keepdims=True) + EPS) * g


def _dot(a, b):
    return jnp.dot(a, b, preferred_element_type=F32)


def _params(sem):
    return pltpu.CompilerParams(dimension_semantics=sem, vmem_limit_bytes=VMEM_LIMIT)


def _halo_specs(tokens, seq):
    per = tokens // HALO
    last = seq // HALO - 1
    main = pl.BlockSpec((1, tokens, D_MODEL), lambda b, i, *_: (b, i, 0))
    prev = pl.BlockSpec((1, HALO, D_MODEL), lambda b, i, *_: (b, jnp.maximum(i * per - 1, 0), 0))
    nxt = pl.BlockSpec((1, HALO, D_MODEL), lambda b, i, *_: (b, jnp.minimum((i + 1) * per, last), 0))
    return main, prev, nxt


def _pool_kernel(xm_ref, xp_ref, xn_ref, g_ref, w_ref, b_ref, sc_ref, o_ref, *, seq, tp):
    i = pl.program_id(1)
    n = tp + 2 * HALO
    xm = xm_ref[0]
    g_all = g_ref[...]
    up = jnp.where(i > 0, _rms(xp_ref[0], g_all), 0.0)
    un = jnp.where(i < pl.num_programs(1) - 1, _rms(xn_ref[0], g_all), 0.0)
    u = jnp.concatenate([up, _rms(xm, g_all), un], axis=0)
    posm = i * tp + lax.broadcasted_iota(jnp.int32, (tp, 1), 0)
    for g, w in enumerate(POOL_WINDOWS):
        cols = slice(g * GROUP_DIM, (g + 1) * GROUP_DIM)
        ug = u[:, cols]
        a = ug + pltpu.roll(ug, 1, 0)
        width = 2
        while width < w:
            a = a + pltpu.roll(a, width, 0)
            width *= 2
        s = a if w == 2 else pltpu.roll(a, n - (w // 2 - 1), 0)
        lo = jnp.clip(posm - w // 2, 0, seq)
        hi = jnp.clip(posm - w // 2 + w, 0, seq)
        inv_cnt = 1.0 / (hi - lo).astype(F32)
        d = s[HALO:HALO + tp] * inv_cnt - ug[HALO:HALO + tp]
        y = _dot(d.astype(BF16), w_ref[g]) + b_ref[g]
        o_ref[0, :, cols] = xm[:, cols] + y * sc_ref[:, cols]


def _pool_layer(x, g, w, b, sc):
    B, S, D = x.shape
    tp = min(POOL_TOKENS, S)
    main, prev, nxt = _halo_specs(tp, S)
    const2 = lambda b_, i: (0, 0)
    const3 = lambda b_, i: (0, 0, 0)
    return pl.pallas_call(
        functools.partial(_pool_kernel, seq=S, tp=tp),
        grid=(B, S // tp),
        in_specs=[main, prev, nxt,
                  pl.BlockSpec((1, D), const2),
                  pl.BlockSpec((N_POOL_GROUPS, GROUP_DIM, GROUP_DIM), const3),
                  pl.BlockSpec((N_POOL_GROUPS, 1, GROUP_DIM), const3),
                  pl.BlockSpec((1, D), const2)],
        out_specs=pl.BlockSpec((1, tp, D), lambda b_, i: (b_, i, 0)),
        out_shape=jax.ShapeDtypeStruct((B, S, D), F32),
        compiler_params=_params(("parallel", "parallel")),
        name="pool_layer",
    )(x, x, x, g, w, b, sc)


def _gelu(x):
    return 0.5 * x * (1.0 + lax.erf(x * np.float32(np.sqrt(0.5))))


def _ffn_kernel(x_hbm, nw_ref, wg_ref, wv_ref, cw_ref, cb_ref, wd_ref, fn_ref,
                o_ref, xbuf, u_ref, sem, *, seq, tm, final_norm):
    b, i, j = pl.program_id(0), pl.program_id(1), pl.program_id(2)
    ni = pl.num_programs(1)
    n_tiles = pl.num_programs(0) * ni
    t = b * ni + i
    n = tm + 2 * HALO
    last = pl.num_programs(2) - 1

    def tile_copies(tile):
        bt, it = tile // ni, tile % ni
        pieces = ((jnp.maximum(it * tm - HALO, 0), 0, HALO),
                  (it * tm, HALO, tm),
                  (jnp.minimum((it + 1) * tm, seq - HALO), HALO + tm, HALO))
        return [pltpu.make_async_copy(x_hbm.at[bt, pl.ds(pl.multiple_of(src, HALO), size)],
                                      xbuf.at[pl.ds(dst, size)], sem.at[k])
                for k, (src, dst, size) in enumerate(pieces)]

    def chunk(c):
        cs = slice(c * FFN_CHUNK, (c + 1) * FFN_CHUNK)
        gate = _dot(u_ref[...], wg_ref[0, :, cs])
        val = _dot(u_ref[0:tm], wv_ref[0, :, cs])
        cw = cw_ref[0, :, cs]
        conv = (pltpu.roll(gate, 1, 0)[:tm] * cw[0:1] + gate[:tm] * cw[1:2]
                + pltpu.roll(gate, n - 1, 0)[:tm] * cw[2:3] + cb_ref[0, :, cs])
        h = _gelu(conv) * val
        o_ref[0] += _dot(h.astype(BF16), wd_ref[0, cs, :])

    @pl.when(j == 0)
    def _():
        @pl.when(t == 0)
        def _():
            for c in tile_copies(0):
                c.start()

        for c in tile_copies(t):
            c.wait()
        xm = xbuf[HALO:HALO + tm]
        g = nw_ref[0]
        u_ref[0:tm] = _rms(xm, g).astype(BF16)
        un = jnp.where(i < ni - 1, _rms(xbuf[HALO + tm:n], g), 0.0)
        up = jnp.where(i > 0, _rms(xbuf[0:HALO], g), 0.0)
        u_ref[tm:n] = jnp.concatenate([un, up], axis=0).astype(BF16)
        o_ref[0] = xm

    @pl.when((j == 1) & (t + 1 < n_tiles))
    def _():
        for c in tile_copies(t + 1):
            c.start()

    @pl.when(j < last)
    def _():
        chunk(0)
        chunk(1)

    @pl.when(j == last)
    def _():
        chunk(0)
        if final_norm:
            o_ref[0] = _rms(o_ref[0], fn_ref[...])


def _ffn_layer(x, layer, nw, wg, wv, cw, cb, w_down, fn, *, final_norm):
    B, S, D = x.shape
    tm = min(FFN_TOKENS, S)
    tf = 2 * FFN_CHUNK
    nj = pl.cdiv(D_FF, tf)
    assert D_FF % FFN_CHUNK == 0 and (D_FF // FFN_CHUNK) % 2 == 1 and nj >= 2 and S % tm == 0
    return pl.pallas_call(
        functools.partial(_ffn_kernel, seq=S, tm=tm, final_norm=final_norm),
        grid=(B, S // tm, nj),
        in_specs=[pl.BlockSpec(memory_space=pl.ANY),
                  pl.BlockSpec((1, 1, D), lambda b_, i, j: (layer, 0, 0)),
                  pl.BlockSpec((1, D, tf), lambda b_, i, j: (layer, 0, j)),
                  pl.BlockSpec((1, D, tf), lambda b_, i, j: (layer, 0, j)),
                  pl.BlockSpec((1, 3, tf), lambda b_, i, j: (layer, 0, j)),
                  pl.BlockSpec((1, 1, tf), lambda b_, i, j: (layer, 0, j)),
                  pl.BlockSpec((1, tf, D), lambda b_, i, j: (layer, j, 0)),
                  pl.BlockSpec((1, D), lambda b_, i, j: (0, 0))],
        out_specs=pl.BlockSpec((1, tm, D), lambda b_, i, j: (b_, i, 0)),
        out_shape=jax.ShapeDtypeStruct((B, S, D), F32),
        scratch_shapes=[pltpu.VMEM((tm + 2 * HALO, D), F32),
                        pltpu.VMEM((tm + 2 * HALO, D), BF16),
                        pltpu.SemaphoreType.DMA((3,))],
        compiler_params=_params(("arbitrary", "arbitrary", "arbitrary")),
        name="ffn_final" if final_norm else "ffn",
    )(x, nw, wg, wv, cw, cb, w_down, fn)


def _qkv_kernel(x_ref, nw_ref, w_ref, b_ref, s_ref, o_ref, u_ref):
    def project():
        res = ((_dot(u_ref[...], w_ref[...]) + b_ref[...]) * s_ref[...]).astype(BF16)
        for hh in range(o_ref.shape[1]):
            o_ref[0, hh] = res[:, hh * HEAD_DIM:(hh + 1) * HEAD_DIM]

    @pl.when(pl.program_id(2) == 0)
    def _():
        u_ref[...] = _rms(x_ref[0], nw_ref[...]).astype(BF16)
        project()

    @pl.when(pl.program_id(2) > 0)
    def _():
        project()


def _qkv_layer(x, nw, w, b, s):
    B, S, D = x.shape
    tm = min(QKV_TOKENS, S)
    tn = QKV_CHUNK
    N = w.shape[1]
    return pl.pallas_call(
        _qkv_kernel,
        grid=(B, S // tm, N // tn),
        in_specs=[pl.BlockSpec((1, tm, D), lambda b_, i, j: (b_, i, 0)),
                  pl.BlockSpec((1, D), lambda b_, i, j: (0, 0)),
                  pl.BlockSpec((D, tn), lambda b_, i, j: (0, j)),
                  pl.BlockSpec((1, tn), lambda b_, i, j: (0, j)),
                  pl.BlockSpec((1, tn), lambda b_, i, j: (0, j))],
        out_specs=pl.BlockSpec((1, tn // HEAD_DIM, tm, HEAD_DIM), lambda b_, i, j: (b_, j, i, 0)),
        out_shape=jax.ShapeDtypeStruct((B, N // HEAD_DIM, S, HEAD_DIM), BF16),
        scratch_shapes=[pltpu.VMEM((tm, D), BF16)],
        compiler_params=_params(("parallel", "parallel", "arbitrary")),
        name="qkv",
    )(x, nw, w, b, s)


def _row_patterns():
    rows = 4 * ATT_ROWS
    kstart, tid = np.zeros((3, ATT_ROWS), int), np.zeros((3, ATT_ROWS), int)
    for v, r0 in enumerate((0, ATT_ROWS, rows - ATT_ROWS)):
        kb = np.clip(r0 - WIN_H // 2, 0, rows - ATT_KEY_ROWS)
        r = r0 + np.arange(ATT_ROWS)
        rs = np.clip(r - WIN_H // 2, 0, rows - WIN_H)
        kstart[v] = rs - kb
        tid[v] = rs - r + WIN_H - 1
    assert kstart.min() >= 0 and kstart.max() + WIN_H <= ATT_KEY_ROWS
    assert tid.min() >= 0 and tid.max() < WIN_H
    return kstart, tid


_ROW_KSTART, _ROW_TID = _row_patterns()


def _attn_bias_tables(rpb):
    H = rpb.shape[0]
    c = np.arange(GRID_W)
    cstart = np.clip(c - WIN_W // 2, 0, GRID_W - WIN_W)
    col_ok = (c[None, :] >= cstart[:, None]) & (c[None, :] < cstart[:, None] + WIN_W)
    span = 2 * GRID_W
    vi = np.clip(np.arange(span) - (GRID_W - 1), -(WIN_W - 1), WIN_W - 1) + WIN_W - 1
    vp = rpb[:, :, vi]
    skew = jnp.broadcast_to(vp[:, :, None, :], (H, 2 * WIN_H - 1, GRID_W, span))
    skew = skew.reshape(H, 2 * WIN_H - 1, GRID_W * span)[:, :, :GRID_W * (span - 1)]
    t = skew.reshape(H, 2 * WIN_H - 1, GRID_W, span - 1)[..., GRID_W - 1:]
    t = jnp.where(col_ok, t * LOG2_E, NEG_INF)
    t = t.transpose(0, 2, 1, 3).reshape(H, GRID_W, (2 * WIN_H - 1) * GRID_W)
    return jnp.stack([t[:, :, s * GRID_W:(s + WIN_H) * GRID_W] for s in range(WIN_H)], axis=1)


def _attn_kernel(q_ref, k_ref, v_ref, bias_ref, o_ref):
    i = pl.program_id(2)
    first = i == 0
    last = i == pl.num_programs(2) - 1
    nq = GRID_W
    nk = WIN_H * GRID_W

    def pick(tab, r):
        return jnp.where(first, int(tab[0, r]), jnp.where(last, int(tab[2, r]), int(tab[1, r])))

    def scores(hh, r):
        off = pl.multiple_of(pick(_ROW_KSTART, r) * GRID_W, GRID_W)
        q = q_ref[0, hh, r * nq:(r + 1) * nq, :]
        s = lax.dot_general(q, k_ref[0, hh, pl.ds(off, nk), :], (((1,), (1,)), ((), ())),
                            preferred_element_type=F32)
        return s + bias_ref[hh, pick(_ROW_TID, r)]

    def softmax(s):
        p = jnp.exp2(s - jnp.max(s, axis=-1, keepdims=True))
        return p.astype(BF16), jnp.sum(p, axis=-1, keepdims=True)

    def weighted(hh, r, p, l):
        off = pl.multiple_of(pick(_ROW_KSTART, r) * GRID_W, GRID_W)
        o = _dot(p, v_ref[0, hh, pl.ds(off, nk), :]) / l
        o_ref[0, hh, r * nq:(r + 1) * nq, :] = o.astype(BF16)

    work = [(hh, r) for hh in range(q_ref.shape[1]) for r in range(ATT_ROWS)]
    s_live, p_live = {}, {}
    d1, d2 = ATT_SKEW, 2 * ATT_SKEW
    for t in range(len(work) + d2):
        if t < len(work):
            s_live[t] = scores(*work[t])
        if 0 <= t - d1 < len(work):
            p_live[t - d1] = softmax(s_live.pop(t - d1))
        if 0 <= t - d2 < len(work):
            weighted(*work[t - d2], *p_live.pop(t - d2))


def _attn_layer(qkv, bias):
    B, _, S, _ = qkv.shape
    H, hb = N_HEADS, ATT_HEADS
    tq = ATT_ROWS * GRID_W
    tk = ATT_KEY_ROWS * GRID_W
    halo = (tk - tq) // 2
    nblk = S // tq
    assert nblk >= 3 and S % tq == 0 and tq % halo == 0
    ng = H // hb

    def keys(part):
        return pl.BlockSpec(
            (pl.Element(1), pl.Element(hb), pl.Element(tk), pl.Element(HEAD_DIM)),
            lambda b_, g, i: (b_, (part * ng + g) * hb,
                              halo * jnp.clip((tq // halo) * i - 1, 0, (S - tk) // halo), 0))

    nt = bias.shape[1]
    return pl.pallas_call(
        _attn_kernel,
        grid=(B, ng, nblk),
        in_specs=[pl.BlockSpec((1, hb, tq, HEAD_DIM), lambda b_, g, i: (b_, g, i, 0)),
                  keys(1), keys(2),
                  pl.BlockSpec((hb, nt, GRID_W, WIN_H * GRID_W), lambda b_, g, i: (g, 0, 0, 0))],
        out_specs=pl.BlockSpec((1, hb, tq, HEAD_DIM), lambda b_, g, i: (b_, g, i, 0)),
        out_shape=jax.ShapeDtypeStruct((B, H, S, HEAD_DIM), BF16),
        compiler_params=_params(("parallel", "parallel", "arbitrary")),
        name="natten",
    )(qkv, qkv, qkv, bias)


def _wo_kernel(o_ref, x_ref, w_ref, y_ref):
    o = jnp.concatenate([o_ref[0, h] for h in range(o_ref.shape[1])], axis=-1)
    y_ref[0] = x_ref[0] + _dot(o, w_ref[...])


def _wo_layer(o, x, w):
    B, S, D = x.shape
    tm = min(WO_TOKENS, S)
    tile = pl.BlockSpec((1, tm, D), lambda b_, i: (b_, i, 0))
    return pl.pallas_call(
        _wo_kernel,
        grid=(B, S // tm),
        in_specs=[pl.BlockSpec((1, N_HEADS, tm, HEAD_DIM), lambda b_, i: (b_, 0, i, 0)),
                  tile,
                  pl.BlockSpec((D, D), lambda b_, i: (0, 0), pipeline_mode=pl.Buffered(1))],
        out_specs=tile,
        out_shape=jax.ShapeDtypeStruct((B, S, D), F32),
        compiler_params=_params(("parallel", "parallel")),
        name="attn_out",
    )(o, x, w)


def _prep_weights(mix_norm, pool_w, pool_b, pool_scale, attn_w_qkv, attn_b_qkv, attn_rpb, attn_w_o,
                  ffn_norm, ffn_w_up, ffn_conv_w, ffn_conv_b, ffn_w_down, final_norm):
    assert pool_w.shape[0] == 1 and attn_w_qkv.shape[0] == 1 and ffn_w_up.shape[0] == 2
    q_scale = jnp.concatenate([jnp.full((D_MODEL,), HEAD_DIM ** -0.5 * LOG2_E, F32),
                               jnp.ones((2 * D_MODEL,), F32)]).reshape(1, 3 * D_MODEL)
    return dict(
        mix_norm=mix_norm.reshape(-1, 1, D_MODEL),
        pool_w=pool_w[0].astype(BF16),
        pool_b=pool_b[0].reshape(N_POOL_GROUPS, 1, GROUP_DIM),
        pool_scale=pool_scale[0].reshape(1, D_MODEL),
        w_qkv=attn_w_qkv[0].astype(BF16),
        b_qkv=attn_b_qkv[0].reshape(1, 3 * D_MODEL),
        q_scale=q_scale,
        bias=_attn_bias_tables(attn_rpb[0]),
        w_o=attn_w_o[0].astype(BF16),
        ffn=(ffn_norm.reshape(-1, 1, D_MODEL), ffn_w_up[:, :, :D_FF].astype(BF16),
             ffn_w_up[:, :, D_FF:].astype(BF16), ffn_conv_w, ffn_conv_b.reshape(-1, 1, D_FF),
             ffn_w_down.astype(BF16)),
        final_norm=final_norm.reshape(1, D_MODEL),
    )


def _trunk(x, p):
    x = _pool_layer(x, p["mix_norm"][0], p["pool_w"], p["pool_b"], p["pool_scale"])
    x = _ffn_layer(x, 0, *p["ffn"], p["final_norm"], final_norm=False)
    qkv = _qkv_layer(x, p["mix_norm"][1], p["w_qkv"], p["b_qkv"], p["q_scale"])
    o = _attn_layer(qkv, p["bias"])
    x = _wo_layer(o, x, p["w_o"])
    return _ffn_layer(x, 1, *p["ffn"], p["final_norm"], final_norm=True)


def kernel(x_prompt, x_sample, mix_norm, pool_w, pool_b, pool_scale, attn_w_qkv, attn_b_qkv, attn_rpb,
           attn_w_o, ffn_norm, ffn_w_up, ffn_conv_w, ffn_conv_b, ffn_w_down, final_norm):
    p = _prep_weights(mix_norm, pool_w, pool_b, pool_scale, attn_w_qkv, attn_b_qkv, attn_rpb, attn_w_o,
                      ffn_norm, ffn_w_up, ffn_conv_w, ffn_conv_b, ffn_w_down, final_norm)
    return _trunk(x_prompt, p), _trunk(x_sample, p)
```

```python
import functools

import numpy as np
import jax
import jax.numpy as jnp
from jax import lax
from jax.experimental import pallas as pl
from jax.experimental.pallas import tpu as pltpu

D_MODEL = 2048
GRID_W = 64
N_POOL_GROUPS = 4
POOL_WINDOWS = (2, 4, 8, 16)
GROUP_DIM = D_MODEL // N_POOL_GROUPS
N_HEADS = 16
HEAD_DIM = D_MODEL // N_HEADS
WIN_H = 8
WIN_W = 16
D_FF = 5632
EPS = 1e-6
NEG_INF = -1e30
LOG2_E = float(np.log2(np.e))

HALO = 8
POOL_TOKENS = 512
FFN_TOKENS = 1024
FFN_CHUNK = 512
QKV_TOKENS = 1024
QKV_CHUNK = 2048
WO_TOKENS = 1024
ATT_ROWS = 32
ATT_HEADS = 4
ATT_KEY_ROWS = ATT_ROWS + WIN_H
ATT_SKEW = 4
VMEM_LIMIT = 60 * 1024 * 1024

F32 = jnp.float32
BF16 = jnp.bfloat16


def _rms(x, g):
    return x * lax.rsqrt(jnp.mean(x * x, axis=-1, keepdims=True) + EPS) * g


def _dot(a, b):
    return jnp.dot(a, b, preferred_element_type=F32)


def _params(sem):
    return pltpu.CompilerParams(dimension_semantics=sem, vmem_limit_bytes=VMEM_LIMIT)


def _halo_specs(tokens, seq):
    per = tokens // HALO
    last = seq // HALO - 1
    main = pl.BlockSpec((1, tokens, D_MODEL), lambda b, i, *_: (b, i, 0))
    prev = pl.BlockSpec((1, HALO, D_MODEL), lambda b, i, *_: (b, jnp.maximum(i * per - 1, 0), 0))
    nxt = pl.BlockSpec((1, HALO, D_MODEL), lambda b, i, *_: (b, jnp.minimum((i + 1) * per, last), 0))
    return main, prev, nxt


def _pool_kernel(xm_ref, xp_ref, xn_ref, g_ref, w_ref, b_ref, sc_ref, o_ref, *, seq, tp):
    i = pl.program_id(1)
    n = tp + 2 * HALO
    xm = xm_ref[0]
    g_all = g_ref[...]
    up = jnp.where(i > 0, _rms(xp_ref[0], g_all), 0.0)
    un = jnp.where(i < pl.num_programs(1) - 1, _rms(xn_ref[0], g_all), 0.0)
    u = jnp.concatenate([up, _rms(xm, g_all), un], axis=0)
    posm = i * tp + lax.broadcasted_iota(jnp.int32, (tp, 1), 0)
    for g, w in enumerate(POOL_WINDOWS):
        cols = slice(g * GROUP_DIM, (g + 1) * GROUP_DIM)
        ug = u[:, cols]
        a = ug + pltpu.roll(ug, 1, 0)
        width = 2
        while width < w:
            a = a + pltpu.roll(a, width, 0)
            width *= 2
        s = a if w == 2 else pltpu.roll(a, n - (w // 2 - 1), 0)
        lo = jnp.clip(posm - w // 2, 0, seq)
        hi = jnp.clip(posm - w // 2 + w, 0, seq)
        inv_cnt = 1.0 / (hi - lo).astype(F32)
        d = s[HALO:HALO + tp] * inv_cnt - ug[HALO:HALO + tp]
        y = _dot(d.astype(BF16), w_ref[g]) + b_ref[g]
        o_ref[0, :, cols] = xm[:, cols] + y * sc_ref[:, cols]


def _pool_layer(x, g, w, b, sc):
    B, S, D = x.shape
    tp = min(POOL_TOKENS, S)
    main, prev, nxt = _halo_specs(tp, S)
    const2 = lambda b_, i: (0, 0)
    const3 = lambda b_, i: (0, 0, 0)
    return pl.pallas_call(
        functools.partial(_pool_kernel, seq=S, tp=tp),
        grid=(B, S // tp),
        in_specs=[main, prev, nxt,
                  pl.BlockSpec((1, D), const2),
                  pl.BlockSpec((N_POOL_GROUPS, GROUP_DIM, GROUP_DIM), const3),
                  pl.BlockSpec((N_POOL_GROUPS, 1, GROUP_DIM), const3),
                  pl.BlockSpec((1, D), const2)],
        out_specs=pl.BlockSpec((1, tp, D), lambda b_, i: (b_, i, 0)),
        out_shape=jax.ShapeDtypeStruct((B, S, D), F32),
        compiler_params=_params(("parallel", "parallel")),
        name="pool_layer",
    )(x, x, x, g, w, b, sc)


def _gelu(x):
    return 0.5 * x * (1.0 + lax.erf(x * np.float32(np.sqrt(0.5))))


def _ffn_kernel(x_hbm, nw_ref, wg_ref, wv_ref, cw_ref, cb_ref, wd_ref, fn_ref,
                o_ref, xbuf, u_ref, sem, *, seq, tm, final_norm):
    b, i, j = pl.program_id(0), pl.program_id(1), pl.program_id(2)
    ni = pl.num_programs(1)
    n_tiles = pl.num_programs(0) * ni
    t = b * ni + i
    n = tm + 2 * HALO
    last = pl.num_programs(2) - 1

    def tile_copies(tile):
        bt, it = tile // ni, tile % ni
        pieces = ((jnp.maximum(it * tm - HALO, 0), 0, HALO),
                  (it * tm, HALO, tm),
                  (jnp.minimum((it + 1) * tm, seq - HALO), HALO + tm, HALO))
        return [pltpu.make_async_copy(x_hbm.at[bt, pl.ds(pl.multiple_of(src, HALO), size)],
                                      xbuf.at[pl.ds(dst, size)], sem.at[k])
                for k, (src, dst, size) in enumerate(pieces)]

    def chunk(c):
        cs = slice(c * FFN_CHUNK, (c + 1) * FFN_CHUNK)
        gate = _dot(u_ref[...], wg_ref[0, :, cs])
        val = _dot(u_ref[0:tm], wv_ref[0, :, cs])
        cw = cw_ref[0, :, cs]
        conv = (pltpu.roll(gate, 1, 0)[:tm] * cw[0:1] + gate[:tm] * cw[1:2]
                + pltpu.roll(gate, n - 1, 0)[:tm] * cw[2:3] + cb_ref[0, :, cs])
        h = _gelu(conv) * val
        o_ref[0] += _dot(h.astype(BF16), wd_ref[0, cs, :])

    @pl.when(j == 0)
    def _():
        @pl.when(t == 0)
        def _():
            for c in tile_copies(0):
                c.start()

        for c in tile_copies(t):
            c.wait()
        xm = xbuf[HALO:HALO + tm]
        g = nw_ref[0]
        u_ref[0:tm] = _rms(xm, g).astype(BF16)
        un = jnp.where(i < ni - 1, _rms(xbuf[HALO + tm:n], g), 0.0)
        up = jnp.where(i > 0, _rms(xbuf[0:HALO], g), 0.0)
        u_ref[tm:n] = jnp.concatenate([un, up], axis=0).astype(BF16)
        o_ref[0] = xm

    @pl.when((j == 1) & (t + 1 < n_tiles))
    def _():
        for c in tile_copies(t + 1):
            c.start()

    @pl.when(j < last)
    def _():
        chunk(0)
        chunk(1)

    @pl.when(j == last)
    def _():
        chunk(0)
        if final_norm:
            o_ref[0] = _rms(o_ref[0], fn_ref[...])


def _ffn_layer(x, layer, nw, wg, wv, cw, cb, w_down, fn, *, final_norm):
    B, S, D = x.shape
    tm = min(FFN_TOKENS, S)
    tf = 2 * FFN_CHUNK
    nj = pl.cdiv(D_FF, tf)
    assert D_FF % FFN_CHUNK == 0 and (D_FF // FFN_CHUNK) % 2 == 1 and nj >= 2 and S % tm == 0
    return pl.pallas_call(
        functools.partial(_ffn_kernel, seq=S, tm=tm, final_norm=final_norm),
        grid=(B, S // tm, nj),
        in_specs=[pl.BlockSpec(memory_space=pl.ANY),
                  pl.BlockSpec((1, 1, D), lambda b_, i, j: (layer, 0, 0)),
                  pl.BlockSpec((1, D, tf), lambda b_, i, j: (layer, 0, j)),
                  pl.BlockSpec((1, D, tf), lambda b_, i, j: (layer, 0, j)),
                  pl.BlockSpec((1, 3, tf), lambda b_, i, j: (layer, 0, j)),
                  pl.BlockSpec((1, 1, tf), lambda b_, i, j: (layer, 0, j)),
                  pl.BlockSpec((1, tf, D), lambda b_, i, j: (layer, j, 0)),
                  pl.BlockSpec((1, D), lambda b_, i, j: (0, 0))],
        out_specs=pl.BlockSpec((1, tm, D), lambda b_, i, j: (b_, i, 0)),
        out_shape=jax.ShapeDtypeStruct((B, S, D), F32),
        scratch_shapes=[pltpu.VMEM((tm + 2 * HALO, D), F32),
                        pltpu.VMEM((tm + 2 * HALO, D), BF16),
                        pltpu.SemaphoreType.DMA((3,))],
        compiler_params=_params(("arbitrary", "arbitrary", "arbitrary")),
        name="ffn_final" if final_norm else "ffn",
    )(x, nw, wg, wv, cw, cb, w_down, fn)


def _qkv_kernel(x_ref, nw_ref, w_ref, b_ref, s_ref, o_ref, u_ref):
    def project():
        res = ((_dot(u_ref[...], w_ref[...]) + b_ref[...]) * s_ref[...]).astype(BF16)
        for hh in range(o_ref.shape[1]):
            o_ref[0, hh] = res[:, hh * HEAD_DIM:(hh + 1) * HEAD_DIM]

    @pl.when(pl.program_id(2) == 0)
    def _():
        u_ref[...] = _rms(x_ref[0], nw_ref[...]).astype(BF16)
        project()

    @pl.when(pl.program_id(2) > 0)
    def _():
        project()


def _qkv_layer(x, nw, w, b, s):
    B, S, D = x.shape
    tm = min(QKV_TOKENS, S)
    tn = QKV_CHUNK
    N = w.shape[1]
    return pl.pallas_call(
        _qkv_kernel,
        grid=(B, S // tm, N // tn),
        in_specs=[pl.BlockSpec((1, tm, D), lambda b_, i, j: (b_, i, 0)),
                  pl.BlockSpec((1, D), lambda b_, i, j: (0, 0)),
                  pl.BlockSpec((D, tn), lambda b_, i, j: (0, j)),
                  pl.BlockSpec((1, tn), lambda b_, i, j: (0, j)),
                  pl.BlockSpec((1, tn), lambda b_, i, j: (0, j))],
        out_specs=pl.BlockSpec((1, tn // HEAD_DIM, tm, HEAD_DIM), lambda b_, i, j: (b_, j, i, 0)),
        out_shape=jax.ShapeDtypeStruct((B, N // HEAD_DIM, S, HEAD_DIM), BF16),
        scratch_shapes=[pltpu.VMEM((tm, D), BF16)],
        compiler_params=_params(("parallel", "parallel", "arbitrary")),
        name="qkv",
    )(x, nw, w, b, s)


def _row_patterns():
    rows = 4 * ATT_ROWS
    kstart, tid = np.zeros((3, ATT_ROWS), int), np.zeros((3, ATT_ROWS), int)
    for v, r0 in enumerate((0, ATT_ROWS, rows - ATT_ROWS)):
        kb = np.clip(r0 - WIN_H // 2, 0, rows - ATT_KEY_ROWS)
        r = r0 + np.arange(ATT_ROWS)
        rs = np.clip(r - WIN_H // 2, 0, rows - WIN_H)
        kstart[v] = rs - kb
        tid[v] = rs - r + WIN_H - 1
    assert kstart.min() >= 0 and kstart.max() + WIN_H <= ATT_KEY_ROWS
    assert tid.min() >= 0 and tid.max() < WIN_H
    return kstart, tid


_ROW_KSTART, _ROW_TID = _row_patterns()


def _attn_bias_tables(rpb):
    H = rpb.shape[0]
    c = np.arange(GRID_W)
    cstart = np.clip(c - WIN_W // 2, 0, GRID_W - WIN_W)
    col_ok = (c[None, :] >= cstart[:, None]) & (c[None, :] < cstart[:, None] + WIN_W)
    span = 2 * GRID_W
    vi = np.clip(np.arange(span) - (GRID_W - 1), -(WIN_W - 1), WIN_W - 1) + WIN_W - 1
    vp = rpb[:, :, vi]
    skew = jnp.broadcast_to(vp[:, :, None, :], (H, 2 * WIN_H - 1, GRID_W, span))
    skew = skew.reshape(H, 2 * WIN_H - 1, GRID_W * span)[:, :, :GRID_W * (span - 1)]
    t = skew.reshape(H, 2 * WIN_H - 1, GRID_W, span - 1)[..., GRID_W - 1:]
    t = jnp.where(col_ok, t * LOG2_E, NEG_INF)
    t = t.transpose(0, 2, 1, 3).reshape(H, GRID_W, (2 * WIN_H - 1) * GRID_W)
    return jnp.stack([t[:, :, s * GRID_W:(s + WIN_H) * GRID_W] for s in range(WIN_H)], axis=1)


def _attn_kernel(q_ref, k_ref, v_ref, bias_ref, o_ref):
    i = pl.program_id(2)
    first = i == 0
    last = i == pl.num_programs(2) - 1
    nq = GRID_W
    nk = WIN_H * GRID_W

    def pick(tab, r):
        return jnp.where(first, int(tab[0, r]), jnp.where(last, int(tab[2, r]), int(tab[1, r])))

    def scores(hh, r):
        off = pl.multiple_of(pick(_ROW_KSTART, r) * GRID_W, GRID_W)
        q = q_ref[0, hh, r * nq:(r + 1) * nq, :]
        s = lax.dot_general(q, k_ref[0, hh, pl.ds(off, nk), :], (((1,), (1,)), ((), ())),
                            preferred_element_type=F32)
        return s + bias_ref[hh, pick(_ROW_TID, r)]

    def softmax(s):
        p = jnp.exp2(s - jnp.max(s, axis=-1, keepdims=True))
        return p.astype(BF16), jnp.sum(p, axis=-1, keepdims=True)

    def weighted(hh, r, p, l):
        off = pl.multiple_of(pick(_ROW_KSTART, r) * GRID_W, GRID_W)
        o = _dot(p, v_ref[0, hh, pl.ds(off, nk), :]) / l
        o_ref[0, hh, r * nq:(r + 1) * nq, :] = o.astype(BF16)

    work = [(hh, r) for hh in range(q_ref.shape[1]) for r in range(ATT_ROWS)]
    s_live, p_live = {}, {}
    d1, d2 = ATT_SKEW, 2 * ATT_SKEW
    for t in range(len(work) + d2):
        if t < len(work):
            s_live[t] = scores(*work[t])
        if 0 <= t - d1 < len(work):
            p_live[t - d1] = softmax(s_live.pop(t - d1))
        if 0 <= t - d2 < len(work):
            weighted(*work[t - d2], *p_live.pop(t - d2))


def _attn_layer(qkv, bias):
    B, _, S, _ = qkv.shape
    H, hb = N_HEADS, ATT_HEADS
    tq = ATT_ROWS * GRID_W
    tk = ATT_KEY_ROWS * GRID_W
    halo = (tk - tq) // 2
    nblk = S // tq
    assert nblk >= 3 and S % tq == 0 and tq % halo == 0
    ng = H // hb

    def keys(part):
        return pl.BlockSpec(
            (pl.Element(1), pl.Element(hb), pl.Element(tk), pl.Element(HEAD_DIM)),
            lambda b_, g, i: (b_, (part * ng + g) * hb,
                              halo * jnp.clip((tq // halo) * i - 1, 0, (S - tk) // halo), 0))

    nt = bias.shape[1]
    return pl.pallas_call(
        _attn_kernel,
        grid=(B, ng, nblk),
        in_specs=[pl.BlockSpec((1, hb, tq, HEAD_DIM), lambda b_, g, i: (b_, g, i, 0)),
                  keys(1), keys(2),
                  pl.BlockSpec((hb, nt, GRID_W, WIN_H * GRID_W), lambda b_, g, i: (g, 0, 0, 0))],
        out_specs=pl.BlockSpec((1, hb, tq, HEAD_DIM), lambda b_, g, i: (b_, g, i, 0)),
        out_shape=jax.ShapeDtypeStruct((B, H, S, HEAD_DIM), BF16),
        compiler_params=_params(("parallel", "parallel", "arbitrary")),
        name="natten",
    )(qkv, qkv, qkv, bias)


def _wo_kernel(o_ref, x_ref, w_ref, y_ref):
    o = jnp.concatenate([o_ref[0, h] for h in range(o_ref.shape[1])], axis=-1)
    y_ref[0] = x_ref[0] + _dot(o, w_ref[...])


def _wo_layer(o, x, w):
    B, S, D = x.shape
    tm = min(WO_TOKENS, S)
    tile = pl.BlockSpec((1, tm, D), lambda b_, i: (b_, i, 0))
    return pl.pallas_call(
        _wo_kernel,
        grid=(B, S // tm),
        in_specs=[pl.BlockSpec((1, N_HEADS, tm, HEAD_DIM), lambda b_, i: (b_, 0, i, 0)),
                  tile,
                  pl.BlockSpec((D, D), lambda b_, i: (0, 0), pipeline_mode=pl.Buffered(1))],
        out_specs=tile,
        out_shape=jax.ShapeDtypeStruct((B, S, D), F32),
        compiler_params=_params(("parallel", "parallel")),
        name="attn_out",
    )(o, x, w)


def _prep_weights(mix_norm, pool_w, pool_b, pool_scale, attn_w_qkv, attn_b_qkv, attn_rpb, attn_w_o,
                  ffn_norm, ffn_w_up, ffn_conv_w, ffn_conv_b, ffn_w_down, final_norm):
    assert pool_w.shape[0] == 1 and attn_w_qkv.shape[0] == 1 and ffn_w_up.shape[0] == 2
    q_scale = jnp.concatenate([jnp.full((D_MODEL,), HEAD_DIM ** -0.5 * LOG2_E, F32),
                               jnp.ones((2 * D_MODEL,), F32)]).reshape(1, 3 * D_MODEL)
    return dict(
        mix_norm=mix_norm.reshape(-1, 1, D_MODEL),
        pool_w=pool_w[0].astype(BF16),
        pool_b=pool_b[0].reshape(N_POOL_GROUPS, 1, GROUP_DIM),
        pool_scale=pool_scale[0].reshape(1, D_MODEL),
        w_qkv=attn_w_qkv[0].astype(BF16),
        b_qkv=attn_b_qkv[0].reshape(1, 3 * D_MODEL),
        q_scale=q_scale,
        bias=_attn_bias_tables(attn_rpb[0]),
        w_o=attn_w_o[0].astype(BF16),
        ffn=(ffn_norm.reshape(-1, 1, D_MODEL), ffn_w_up[:, :, :D_FF].astype(BF16),
             ffn_w_up[:, :, D_FF:].astype(BF16), ffn_conv_w, ffn_conv_b.reshape(-1, 1, D_FF),
             ffn_w_down.astype(BF16)),
        final_norm=final_norm.reshape(1, D_MODEL),
    )


def _trunk(x, p):
    x = _pool_layer(x, p["mix_norm"][0], p["pool_w"], p["pool_b"], p["pool_scale"])
    x = _ffn_layer(x, 0, *p["ffn"], p["final_norm"], final_norm=False)
    qkv = _qkv_layer(x, p["mix_norm"][1], p["w_qkv"], p["b_qkv"], p["q_scale"])
    o = _attn_layer(qkv, p["bias"])
    x = _wo_layer(o, x, p["w_o"])
    return _ffn_layer(x, 1, *p["ffn"], p["final_norm"], final_norm=True)


def kernel(x_prompt, x_sample, mix_norm, pool_w, pool_b, pool_scale, attn_w_qkv, attn_b_qkv, attn_rpb,
           attn_w_o, ffn_norm, ffn_w_up, ffn_conv_w, ffn_conv_b, ffn_w_down, final_norm):
    p = _prep_weights(mix_norm, pool_w, pool_b, pool_scale, attn_w_qkv, attn_b_qkv, attn_rpb, attn_w_o,
                      ffn_norm, ffn_w_up, ffn_conv_w, ffn_conv_b, ffn_w_down, final_norm)
    return _trunk(x_prompt, p), _trunk(x_sample, p)
```
